```python
import math
import jax
import jax.numpy as jnp
from jax import lax
import numpy as np

D_MODEL = 2048
BATCH = 1
SEQ = 8192
DEPTH = 4

GLA_HEADS = 8
GLA_DK = 64
GLA_DV = 128
GLA_GATE_RANK = 16
GLA_GATE_NORM = 16.0
MOBA_HEADS = 8
MOBA_HEAD_DIM = 128
MOBA_BLOCK = 256
MOBA_TOPK = 3
MOBA_Q_CHUNK = 64
GDN_HEADS = 8
GDN_HEAD_DIM = 128
GDN_CONV = 4
S5_CHANNELS = 1024
S5_GROUP_CH = 16
S5_GROUPS = S5_CHANNELS // S5_GROUP_CH
S5_STATE = 64
LIN_CHUNK = 64
D_FF = 5632
FFN_CONV = 3
DEEPNORM_ALPHA = (2 * DEPTH) ** 0.25
DEEPNORM_BETA = (8 * DEPTH) ** -0.25
N_EVEN = (DEPTH + 1) // 2
N_ODD = DEPTH // 2
EVEN_SIZES = (GLA_HEADS * GLA_DK, GLA_HEADS * GLA_DK, GLA_HEADS * GLA_DV, GLA_GATE_RANK, GLA_HEADS * GLA_DV,
              MOBA_HEADS * MOBA_HEAD_DIM, MOBA_HEADS * MOBA_HEAD_DIM, MOBA_HEADS * MOBA_HEAD_DIM)
ODD_SIZES = (3 * GDN_HEADS * GDN_HEAD_DIM, GDN_HEADS * GDN_HEAD_DIM, GDN_HEADS, GDN_HEADS, S5_CHANNELS)
EVEN_IN = sum(EVEN_SIZES)
ODD_IN = sum(ODD_SIZES)
EVEN_OUT = GLA_HEADS * GLA_DV + MOBA_HEADS * MOBA_HEAD_DIM
ODD_OUT = GDN_HEADS * GDN_HEAD_DIM + S5_CHANNELS
LN_EPS = 1e-5
RMS_EPS = 1e-6

kernel_name = "hybrid_gla_moba_gdn_s5_deepnorm"


def split_cols(h, sizes):
    return jnp.split(h, np.cumsum(sizes)[:-1].tolist(), axis=-1)


def layer_norm(x, g, b):
    xf = x.astype(jnp.float32)
    mu = jnp.mean(xf, -1, keepdims=True)
    var = jnp.mean(jnp.square(xf - mu), -1, keepdims=True)
    return ((xf - mu) * lax.rsqrt(var + LN_EPS) * g.astype(jnp.float32) + b.astype(jnp.float32)).astype(x.dtype)


def rms_norm(x, g):
    return x * lax.rsqrt(jnp.mean(jnp.square(x), -1, keepdims=True) + RMS_EPS) * g.astype(jnp.float32)


def l2_normalize(x):
    return x * lax.rsqrt(jnp.sum(jnp.square(x), -1, keepdims=True) + RMS_EPS)


def causal_depthwise_conv(x, w):
    width, seq = w.shape[0], x.shape[1]
    xp = jnp.pad(x, ((0, 0), (width - 1, 0), (0, 0)))
    return sum(xp[:, j:j + seq] * w[j] for j in range(width))


def to_chunks(t, chunk):
    b, s, h = t.shape[:3]
    t = jnp.moveaxis(t, 2, 1)
    return t.reshape((b, h, s // chunk, chunk) + t.shape[3:])


def alibi_slopes(n_heads):
    return jnp.exp2(-8.0 * jnp.arange(1, n_heads + 1, dtype=jnp.float32) / n_heads)


def gla_chunked(q, k, v, log_a):
    bsz, seq, n_heads, dk = q.shape
    dv = v.shape[-1]
    q, k, v, log_a = (to_chunks(t, LIN_CHUNK) for t in (q, k, v, log_a))
    cum = jnp.cumsum(log_a, axis=3)
    q_e = q * jnp.exp(cum)
    k_e = k * jnp.exp(-cum)
    causal = jnp.tril(jnp.ones((LIN_CHUNK, LIN_CHUNK), bool))
    attn = jnp.where(causal, jnp.einsum('bhncd,bhnsd->bhncs', q_e, k_e), 0.0)
    cum_last = cum[:, :, :, -1]
    kv = jnp.einsum('bhncd,bhncv->bhndv', k * jnp.exp(cum_last[:, :, :, None] - cum), v)

    def step(state, inp):
        decay, kv_n = inp
        return state * decay[..., None] + kv_n, state

    _, s_prev = lax.scan(step, jnp.zeros((bsz, n_heads, dk, dv), jnp.float32),
                         (jnp.moveaxis(jnp.exp(cum_last), 2, 0), jnp.moveaxis(kv, 2, 0)))
    o = jnp.einsum('bhncs,bhnsv->bhncv', attn, v) + jnp.einsum('bhncd,nbhdv->bhncv', q_e, s_prev)
    return o.transpose(0, 2, 3, 1, 4).reshape(bsz, seq, n_heads, dv)


def gated_delta_chunked(q, k, v, g, beta):
    bsz, seq, n_heads, dk = q.shape
    dv = v.shape[-1]
    q, k, v, g, beta = (to_chunks(t, LIN_CHUNK) for t in (q * dk ** -0.5, k, v, g, beta))
    gc = jnp.cumsum(g, axis=-1)
    causal = jnp.tril(jnp.ones((LIN_CHUNK, LIN_CHUNK), bool))
    strict = jnp.tril(jnp.ones((LIN_CHUNK, LIN_CHUNK), bool), -1)
    diff = gc[..., :, None] - gc[..., None, :]
    decay = jnp.where(causal, jnp.exp(jnp.where(causal, diff, 0.0)), 0.0)
    k_beta = k * beta[..., None]
    lower = jnp.where(strict, jnp.einsum('bhncd,bhnsd->bhncs', k_beta, k) * decay, 0.0)
    rhs = jnp.concatenate([v * beta[..., None], k_beta * jnp.exp(gc)[..., None]], axis=-1)
    sol = lax.linalg.triangular_solve(lower + jnp.eye(LIN_CHUNK, dtype=jnp.float32), rhs,
                                      left_side=True, lower=True, unit_diagonal=True)
    u, w = sol[..., :dv], sol[..., dv:]
    attn = jnp.einsum('bhncd,bhnsd->bhncs', q, k) * decay
    q_dec = q * jnp.exp(gc)[..., None]
    g_last = gc[..., -1]
    k_dec = k * jnp.exp(g_last[..., None] - gc)[..., None]

    def step(state, inp):
        q_n, k_n, u_n, w_n, a_n, gl = inp
        v_new = u_n - jnp.einsum('bhcd,bhdv->bhcv', w_n, state)
        o = jnp.einsum('bhcd,bhdv->bhcv', q_n, state) + jnp.einsum('bhcs,bhsv->bhcv', a_n, v_new)
        state = state * jnp.exp(gl)[..., None, None] + jnp.einsum('bhcd,bhcv->bhdv', k_n, v_new)
        return state, o

    xs = tuple(jnp.moveaxis(t, 2, 0) for t in (q_dec, k_dec, u, w, attn, g_last))
    _, o = lax.scan(step, jnp.zeros((bsz, n_heads, dk, dv), jnp.float32), xs)
    return o.transpose(1, 0, 3, 2, 4).reshape(bsz, seq, n_heads, dv)


def moba_attention(q, k, v):
    bsz, seq, n_heads, hd = q.shape
    blk_len, q_len = MOBA_BLOCK, MOBA_Q_CHUNK
    n_blk = -(-seq // blk_len)
    s_pad = n_blk * blk_len
    pad = ((0, 0), (0, s_pad - seq), (0, 0), (0, 0))
    q, k, v = (jnp.pad(t, pad).transpose(0, 2, 1, 3) for t in (q, k, v))
    scale = hd ** -0.5
    slopes = alibi_slopes(n_heads)
    k_blocks = k.reshape(bsz, n_heads, n_blk, blk_len, hd)
    v_blocks = v.reshape(bsz, n_heads, n_blk, blk_len, hd)
    k_mean = jnp.mean(k_blocks.astype(jnp.float32), axis=3)
    gate = jnp.einsum('bhsd,bhnd->bhsn', q.astype(jnp.float32), k_mean)
    q_blk = jnp.arange(s_pad) // blk_len
    gate = jnp.where(jnp.arange(n_blk)[None, :] < q_blk[:, None], gate, -jnp.inf)
    n_sel = min(MOBA_TOPK, n_blk)
    _, sel = lax.top_k(gate, n_sel)
    n_qc = s_pad // q_len
    q_c = q.reshape(bsz, n_heads, n_qc, q_len, hd).transpose(2, 0, 1, 3, 4)
    sel_c = sel.reshape(bsz, n_heads, n_qc, q_len, n_sel).transpose(2, 0, 1, 3, 4)
    b_idx = jnp.arange(bsz)[:, None, None, None]
    h_idx = jnp.arange(n_heads)[None, :, None, None]
    offs = jnp.arange(blk_len)

    def attend(args):
        qc, sc, c = args
        t = c * q_len + jnp.arange(q_len)
        own = (c * q_len) // blk_len
        k_sel = k_blocks[b_idx, h_idx, sc]
        v_sel = v_blocks[b_idx, h_idx, sc]
        s_sel = jnp.einsum('bhqd,bhqnld->bhqnl', qc, k_sel).astype(jnp.float32) * scale
        dist_sel = (t[:, None, None] - (sc[..., None] * blk_len + offs)).astype(jnp.float32)
        s_sel = s_sel - slopes[:, None, None, None] * dist_sel
        s_sel = jnp.where((jnp.arange(n_sel) < own)[:, None], s_sel, -jnp.inf)
        k_own = lax.dynamic_slice_in_dim(k, own * blk_len, blk_len, axis=2)
        v_own = lax.dynamic_slice_in_dim(v, own * blk_len, blk_len, axis=2)
        s_own = jnp.einsum('bhqd,bhld->bhql', qc, k_own).astype(jnp.float32) * scale
        dist_own = t[:, None] - (own * blk_len + offs)[None, :]
        s_own = jnp.where(dist_own >= 0, s_own - slopes[:, None, None] * dist_own.astype(jnp.float32), -jnp.inf)
        scores = jnp.concatenate([s_sel.reshape(bsz, n_heads, q_len, n_sel * blk_len), s_own], axis=-1)
        p = jax.nn.softmax(scores, axis=-1).astype(v.dtype)
        p_sel = p[..., :n_sel * blk_len].reshape(bsz, n_heads, q_len, n_sel, blk_len)
        return (jnp.einsum('bhqnl,bhqnld->bhqd', p_sel, v_sel)
                + jnp.einsum('bhql,bhld->bhqd', p[..., n_sel * blk_len:], v_own))

    o = lax.map(attend, (q_c, sel_c, jnp.arange(n_qc)))
    return o.transpose(1, 0, 3, 2, 4).reshape(bsz, s_pad, n_heads, hd)[:, :seq]


def complex_affine_combine(e1, e2):
    a1r, a1i, b1r, b1i = e1
    a2r, a2i, b2r, b2i = e2
    return (a2r * a1r - a2i * a1i, a2r * a1i + a2i * a1r,
            a2r * b1r - a2i * b1i + b2r, a2r * b1i + a2i * b1r + b2i)


def s5_ssm(u, a_re, a_im, b_re, b_im, c_re, c_im, d, log_step):
    a_re, a_im, b_re, b_im, c_re, c_im, d = (t.astype(jnp.float32) for t in (a_re, a_im, b_re, b_im, c_re, c_im, d))
    step = jnp.exp(log_step.astype(jnp.float32))[:, None]
    mag = jnp.exp(a_re * step)
    ab_re = mag * jnp.cos(a_im * step)
    ab_im = mag * jnp.sin(a_im * step)
    den = jnp.square(a_re) + jnp.square(a_im)
    z_re = ((ab_re - 1.0) * a_re + ab_im * a_im) / den
    z_im = (ab_im * a_re - (ab_re - 1.0) * a_im) / den
    bb_re = z_re[..., None] * b_re - z_im[..., None] * b_im
    bb_im = z_re[..., None] * b_im + z_im[..., None] * b_re
    bu_re = jnp.einsum('gpc,bsgc->bsgp', bb_re, u)
    bu_im = jnp.einsum('gpc,bsgc->bsgp', bb_im, u)
    elems = (jnp.broadcast_to(ab_re, bu_re.shape), jnp.broadcast_to(ab_im, bu_im.shape), bu_re, bu_im)
    _, _, x_re, x_im = lax.associative_scan(complex_affine_combine, elems, axis=1)
    return (jnp.einsum('gcp,bsgp->bsgc', c_re, x_re) - jnp.einsum('gcp,bsgp->bsgc', c_im, x_im) + d * u)


def even_mixer(x, w_in, w_gate2, b_gate, norm_g, w_out):
    bsz, seq, _ = x.shape
    gq, gk, gv, g_lr, gr, mq, mk, mv = split_cols(x @ w_in, EVEN_SIZES)
    kshape = (bsz, seq, GLA_HEADS, GLA_DK)
    vshape = (bsz, seq, GLA_HEADS, GLA_DV)
    log_a = jax.nn.log_sigmoid((g_lr @ w_gate2 + b_gate).astype(jnp.float32)).reshape(kshape) / GLA_GATE_NORM
    o = gla_chunked(gq.reshape(kshape).astype(jnp.float32) * GLA_DK ** -0.5, gk.reshape(kshape).astype(jnp.float32),
                    gv.reshape(vshape).astype(jnp.float32), log_a)
    o = rms_norm(o, norm_g) * jax.nn.silu(gr.reshape(vshape).astype(jnp.float32))
    gla_out = o.reshape(bsz, seq, GLA_HEADS * GLA_DV).astype(x.dtype)
    mshape = (bsz, seq, MOBA_HEADS, MOBA_HEAD_DIM)
    moba_out = moba_attention(mq.reshape(mshape), mk.reshape(mshape), mv.reshape(mshape))
    moba_out = moba_out.reshape(bsz, seq, MOBA_HEADS * MOBA_HEAD_DIM).astype(x.dtype)
    return jnp.concatenate([gla_out, moba_out], axis=-1) @ w_out


def odd_mixer(x, w_in, conv_w, a_log, dt_bias, norm_g, a_re, a_im, b_re, b_im, c_re, c_im, d, log_step,
              glu_w, glu_b, w_out):
    bsz, seq, _ = x.shape
    qkv, z, beta_in, decay_in, u = split_cols(x @ w_in, ODD_SIZES)
    qkv = jax.nn.silu(causal_depthwise_conv(qkv, conv_w))
    q, k, v = jnp.split(qkv, 3, axis=-1)
    hs = (bsz, seq, GDN_HEADS, GDN_HEAD_DIM)
    q = l2_normalize(q.reshape(hs).astype(jnp.float32))
    k = l2_normalize(k.reshape(hs).astype(jnp.float32))
    v = v.reshape(hs).astype(jnp.float32)
    beta = jax.nn.sigmoid(beta_in.astype(jnp.float32))
    g = -jnp.exp(a_log.astype(jnp.float32)) * jax.nn.softplus(decay_in.astype(jnp.float32) + dt_bias.astype(jnp.float32))
    o = gated_delta_chunked(q, k, v, g, beta)
    o = rms_norm(o, norm_g) * jax.nn.silu(z.reshape(hs).astype(jnp.float32))
    gdn_out = o.reshape(bsz, seq, GDN_HEADS * GDN_HEAD_DIM).astype(x.dtype)
    y = s5_ssm(u.reshape(bsz, seq, S5_GROUPS, S5_GROUP_CH).astype(jnp.float32),
               a_re, a_im, b_re, b_im, c_re, c_im, d, log_step)
    y = jax.nn.gelu(y.reshape(bsz, seq, S5_CHANNELS)).astype(x.dtype)
    s5_out = y * jax.nn.sigmoid(y @ glu_w + glu_b)
    return jnp.concatenate([gdn_out, s5_out], axis=-1) @ w_out


def conv_ffn(x, w_up, conv_w, w_down):
    h = causal_depthwise_conv(x @ w_up, conv_w)
    gate, val = jnp.split(h, 2, axis=-1)
    return (jax.nn.silu(gate) * val) @ w_down


def setup_inputs(seed: int = 0) -> dict:
    key = jax.random.key(seed)
    keys = iter(jax.random.split(key, 40))

    def normal(shape, scale):
        return jax.random.normal(next(keys), shape, jnp.float32) * scale

    def uniform(shape, lo, hi):
        return jax.random.uniform(next(keys), shape, jnp.float32, lo, hi)

    dt = jnp.exp(uniform((N_ODD, GDN_HEADS), math.log(1e-3), math.log(1e-1)))
    return {
        "x": normal((BATCH, SEQ, D_MODEL), 1.0),
        "even_w_in": normal((N_EVEN, D_MODEL, EVEN_IN), D_MODEL ** -0.5),
        "gla_w_gate2": normal((N_EVEN, GLA_GATE_RANK, GLA_HEADS * GLA_DK), GLA_GATE_RANK ** -0.5),
        "gla_b_gate": normal((N_EVEN, GLA_HEADS * GLA_DK), 0.02),
        "gla_norm_g": 1.0 + normal((N_EVEN, GLA_DV), 0.02),
        "even_w_out": normal((N_EVEN, EVEN_OUT, D_MODEL), DEEPNORM_BETA * EVEN_OUT ** -0.5),
        "odd_w_in": normal((N_ODD, D_MODEL, ODD_IN), D_MODEL ** -0.5),
        "gdn_conv_w": normal((N_ODD, GDN_CONV, 3 * GDN_HEADS * GDN_HEAD_DIM), GDN_CONV ** -0.5),
        "gdn_a_log": jnp.log(uniform((N_ODD, GDN_HEADS), 1.0, 16.0)),
        "gdn_dt_bias": dt + jnp.log(-jnp.expm1(-dt)),
        "gdn_norm_g": 1.0 + normal((N_ODD, GDN_HEAD_DIM), 0.02),
        "s5_a_re": -0.5 + normal((N_ODD, S5_GROUPS, S5_STATE), 0.01),
        "s5_a_im": math.pi * jnp.arange(S5_STATE, dtype=jnp.float32) + normal((N_ODD, S5_GROUPS, S5_STATE), 0.01),
        "s5_b_re": normal((N_ODD, S5_GROUPS, S5_STATE, S5_GROUP_CH), (2 * S5_GROUP_CH) ** -0.5),
        "s5_b_im": normal((N_ODD, S5_GROUPS, S5_STATE, S5_GROUP_CH), (2 * S5_GROUP_CH) ** -0.5),
        "s5_c_re": normal((N_ODD, S5_GROUPS, S5_GROUP_CH, S5_STATE), S5_STATE ** -0.25),
        "s5_c_im": normal((N_ODD, S5_GROUPS, S5_GROUP_CH, S5_STATE), S5_STATE ** -0.25),
        "s5_d": normal((N_ODD, S5_GROUPS, S5_GROUP_CH), 0.5),
        "s5_log_step": uniform((N_ODD, S5_GROUPS), math.log(1e-3), math.log(1e-1)),
        "s5_glu_w": normal((N_ODD, S5_CHANNELS, S5_CHANNELS), S5_CHANNELS ** -0.5),
        "s5_glu_b": normal((N_ODD, S5_CHANNELS), 0.02),
        "odd_w_out": normal((N_ODD, ODD_OUT, D_MODEL), DEEPNORM_BETA * ODD_OUT ** -0.5),
        "ln_mix_g": 1.0 + normal((DEPTH, D_MODEL), 0.02),
        "ln_mix_b": normal((DEPTH, D_MODEL), 0.02),
        "ffn_w_up": normal((DEPTH, D_MODEL, 2 * D_FF), D_MODEL ** -0.5),
        "ffn_conv_w": normal((DEPTH, FFN_CONV, 2 * D_FF), FFN_CONV ** -0.5),
        "ffn_w_down": normal((DEPTH, D_FF, D_MODEL), DEEPNORM_BETA * D_FF ** -0.5),
        "ln_ffn_g": 1.0 + normal((DEPTH, D_MODEL), 0.02),
        "ln_ffn_b": normal((DEPTH, D_MODEL), 0.02),
    }


def reference(x, even_w_in, gla_w_gate2, gla_b_gate, gla_norm_g, even_w_out, odd_w_in, gdn_conv_w, gdn_a_log,
              gdn_dt_bias, gdn_norm_g, s5_a_re, s5_a_im, s5_b_re, s5_b_im, s5_c_re, s5_c_im, s5_d, s5_log_step,
              s5_glu_w, s5_glu_b, odd_w_out, ln_mix_g, ln_mix_b, ffn_w_up, ffn_conv_w, ffn_w_down, ln_ffn_g, ln_ffn_b):
    for i in range(DEPTH):
        j = i // 2
        if i % 2 == 0:
            mix = even_mixer(x, even_w_in[j], gla_w_gate2[j], gla_b_gate[j], gla_norm_g[j], even_w_out[j])
        else:
            mix = odd_mixer(x, odd_w_in[j], gdn_conv_w[j], gdn_a_log[j], gdn_dt_bias[j], gdn_norm_g[j],
                            s5_a_re[j], s5_a_im[j], s5_b_re[j], s5_b_im[j], s5_c_re[j], s5_c_im[j], s5_d[j],
                            s5_log_step[j], s5_glu_w[j], s5_glu_b[j], odd_w_out[j])
        x = layer_norm(DEEPNORM_ALPHA * x + mix, ln_mix_g[i], ln_mix_b[i])
        x = layer_norm(DEEPNORM_ALPHA * x + conv_ffn(x, ffn_w_up[i], ffn_conv_w[i], ffn_w_down[i]),
                       ln_ffn_g[i], ln_ffn_b[i])
    return x
```

```python
import functools
import math

import jax
import jax.numpy as jnp
from jax import lax
from jax.experimental import pallas as pl
from jax.experimental.pallas import tpu as pltpu

F32 = jnp.float32
BF16 = jnp.bfloat16
HIGHEST = lax.Precision.HIGHEST

DEPTH = 4
GLA_HEADS, GLA_DK, GLA_DV, GLA_GATE_RANK, GLA_GATE_NORM = 8, 64, 128, 16, 16.0
MOBA_HEADS, MOBA_HEAD_DIM, MOBA_BLOCK, MOBA_TOPK = 8, 128, 256, 3
GDN_HEADS, GDN_HEAD_DIM, GDN_CONV = 8, 128, 4
S5_CHANNELS, S5_GROUP_CH, S5_STATE = 1024, 16, 64
S5_GROUPS = S5_CHANNELS // S5_GROUP_CH
LIN_CHUNK = 64
D_FF = 5632
FFN_CONV = 3
DEEPNORM_ALPHA = (2 * DEPTH) ** 0.25
LN_EPS = 1e-5
RMS_EPS = 1e-6

LANES = 128
SUBLANES = 8
VMEM_LIMIT_BYTES = 56 * 1024 * 1024
NEG_BIG = -1e30


def _params(semantics):
    return pltpu.CompilerParams(dimension_semantics=semantics, vmem_limit_bytes=VMEM_LIMIT_BYTES)


def _dot(a, b, precision=None):
    return jnp.dot(a, b, preferred_element_type=F32, precision=precision)


def _dot_nt(a, b, precision=None):
    return lax.dot_general(a, b, (((1,), (1,)), ((), ())), preferred_element_type=F32, precision=precision)


def _dot_tn(a, b, precision=None):
    return lax.dot_general(a, b, (((0,), (0,)), ((), ())), preferred_element_type=F32, precision=precision)


def _sigmoid(x):
    return 1.0 / (1.0 + jnp.exp(-x))


def _silu(x):
    return x * _sigmoid(x)


def _chunk_cumsum(x, chunk):
    pos = lax.broadcasted_iota(jnp.int32, x.shape, 0) % chunk
    k = 1
    while k < chunk:
        x = x + jnp.where(pos >= k, pltpu.roll(x, k, 0), 0.0)
        k *= 2
    return x


def _mm_kernel(a_ref, b_ref, o_ref):
    o_ref[...] = _dot(a_ref[...].astype(BF16), b_ref[...]).astype(o_ref.dtype)


def _matmul(a, b, out_dtype, tm, tn):
    m, k = a.shape
    n = b.shape[1]
    return pl.pallas_call(
        _mm_kernel,
        grid=(m // tm, n // tn),
        in_specs=[pl.BlockSpec((tm, k), lambda i, j: (i, 0)),
                  pl.BlockSpec((k, tn), lambda i, j: (0, j))],
        out_specs=pl.BlockSpec((tm, tn), lambda i, j: (i, j)),
        out_shape=jax.ShapeDtypeStruct((m, n), out_dtype),
        compiler_params=_params(("parallel", "parallel")),
        name="matmul",
    )(a, b)


def _mm_ln_kernel(a_ref, b_ref, x_ref, g_ref, bt_ref, of_ref, ob_ref, acc_ref, *, nk):
    k = pl.program_id(1)

    @pl.when(k == 0)
    def _():
        acc_ref[...] = jnp.zeros_like(acc_ref)

    acc_ref[...] += _dot(a_ref[...], b_ref[...])

    @pl.when(k == nk - 1)
    def _():
        y = DEEPNORM_ALPHA * x_ref[...] + acc_ref[...]
        mu = jnp.mean(y, -1, keepdims=True)
        d = y - mu
        var = jnp.mean(d * d, -1, keepdims=True)
        o = d * lax.rsqrt(var + LN_EPS) * g_ref[...] + bt_ref[...]
        of_ref[...] = o
        ob_ref[...] = o.astype(BF16)


def _matmul_res_ln(a, b, x, g, bt, tm, tk):
    m, k = a.shape
    n = b.shape[1]
    nk = k // tk
    return pl.pallas_call(
        functools.partial(_mm_ln_kernel, nk=nk),
        grid=(m // tm, nk),
        in_specs=[pl.BlockSpec((tm, tk), lambda i, kk: (i, kk)),
                  pl.BlockSpec((tk, n), lambda i, kk: (kk, 0)),
                  pl.BlockSpec((tm, n), lambda i, kk: (i, 0)),
                  pl.BlockSpec((1, n), lambda i, kk: (0, 0)),
                  pl.BlockSpec((1, n), lambda i, kk: (0, 0))],
        out_specs=[pl.BlockSpec((tm, n), lambda i, kk: (i, 0)),
                   pl.BlockSpec((tm, n), lambda i, kk: (i, 0))],
        out_shape=[jax.ShapeDtypeStruct((m, n), F32), jax.ShapeDtypeStruct((m, n), BF16)],
        scratch_shapes=[pltpu.VMEM((tm, n), F32)],
        compiler_params=_params(("parallel", "arbitrary")),
        name="matmul_res_ln",
    )(a, b, x, g.reshape(1, n), bt.reshape(1, n))


def _causal_conv_rows(h, w, prev):
    width = w.shape[0]
    out = w[width - 1:width] * h
    top = jnp.concatenate([prev, h[0:SUBLANES]], axis=0)
    out_top = w[width - 1:width] * top
    for s in range(1, width):
        wj = w[width - 1 - s:width - s]
        out = out + wj * pltpu.roll(h, s, 0)
        out_top = out_top + wj * pltpu.roll(top, s, 0)
    return out, out_top[SUBLANES:]


def _ffn_up_kernel(x_ref, wg_ref, wv_ref, cg_ref, cv_ref, o_ref, carry_g, carry_v):
    @pl.when(pl.program_id(1) == 0)
    def _():
        carry_g[...] = jnp.zeros_like(carry_g)
        carry_v[...] = jnp.zeros_like(carry_v)

    x = x_ref[...]
    tm = x.shape[0]
    hg = _dot(x, wg_ref[...])
    hv = _dot(x, wv_ref[...])
    g, g_top = _causal_conv_rows(hg, cg_ref[...], carry_g[...])
    v, v_top = _causal_conv_rows(hv, cv_ref[...], carry_v[...])
    carry_g[...] = hg[tm - SUBLANES:]
    carry_v[...] = hv[tm - SUBLANES:]
    o_ref[...] = (_silu(g) * v).astype(o_ref.dtype)
    o_ref[0:SUBLANES, :] = (_silu(g_top) * v_top).astype(o_ref.dtype)


def _ffn_up(xb, w_up, conv_w, tm, tn):
    m, k = xb.shape
    d_ff = w_up.shape[1] // 2
    nt = d_ff // tn
    return pl.pallas_call(
        _ffn_up_kernel,
        grid=(nt, m // tm),
        in_specs=[pl.BlockSpec((tm, k), lambda j, i: (i, 0)),
                  pl.BlockSpec((k, tn), lambda j, i: (0, j)),
                  pl.BlockSpec((k, tn), lambda j, i: (0, j + nt)),
                  pl.BlockSpec((FFN_CONV, tn), lambda j, i: (0, j)),
                  pl.BlockSpec((FFN_CONV, tn), lambda j, i: (0, j + nt))],
        out_specs=pl.BlockSpec((tm, tn), lambda j, i: (i, j)),
        out_shape=jax.ShapeDtypeStruct((m, d_ff), BF16),
        scratch_shapes=[pltpu.VMEM((SUBLANES, tn), F32), pltpu.VMEM((SUBLANES, tn), F32)],
        compiler_params=_params(("parallel", "arbitrary")),
        name="ffn_up",
    )(xb, w_up, w_up, conv_w, conv_w)


def _gla_kernel(q_ref, k_ref, v_ref, r_ref, glr_ref, wg2_ref, bg_ref, ng_ref, o_ref,
                st_ref, qe_s, ke_s, kd_s, vb_s, ecl_s, *, tt):
    nc = tt // LIN_CHUNK
    hk = GLA_HEADS * GLA_DK

    @pl.when(pl.program_id(0) == 0)
    def _():
        st_ref[...] = jnp.zeros_like(st_ref)

    z = _dot(glr_ref[...], wg2_ref[...], HIGHEST) + bg_ref[...]
    log_a = (jnp.minimum(z, 0.0) - jnp.log(1.0 + jnp.exp(-jnp.abs(z)))) / GLA_GATE_NORM
    cum = _chunk_cumsum(log_a, LIN_CHUNK)
    q = q_ref[...] * (GLA_DK ** -0.5)
    k = k_ref[...]
    qe_s[...] = (q * jnp.exp(cum)).astype(BF16)
    ke_s[...] = (k * jnp.exp(-cum)).astype(BF16)
    for c in range(nc):
        rows = slice(c * LIN_CHUNK, (c + 1) * LIN_CHUNK)
        cl = cum[(c + 1) * LIN_CHUNK - 1:(c + 1) * LIN_CHUNK, :]
        kd_s[rows, :] = (k[rows] * jnp.exp(cl - cum[rows])).astype(BF16)
        ecl_s[c * SUBLANES:(c + 1) * SUBLANES, :] = jnp.broadcast_to(jnp.exp(cl), (SUBLANES, hk))
    vb_s[...] = v_ref[...].astype(BF16)

    ri = lax.broadcasted_iota(jnp.int32, (LIN_CHUNK, LIN_CHUNK), 0)
    ci = lax.broadcasted_iota(jnp.int32, (LIN_CHUNK, LIN_CHUNK), 1)
    causal = ri >= ci
    lane = lax.broadcasted_iota(jnp.int32, (LIN_CHUNK, LANES), 1)
    lo_half = lane < GLA_DK
    ng = ng_ref[...]

    def chunk_body(c, carry):
        r0 = pl.multiple_of(c * LIN_CHUNK, LIN_CHUNK)
        e0 = pl.multiple_of(c * SUBLANES, SUBLANES)
        for h in range(GLA_HEADS):
            pcol = slice((h // 2) * LANES, (h // 2 + 1) * LANES)
            keep = lo_half if h % 2 == 0 else jnp.logical_not(lo_half)
            qm = jnp.where(keep, qe_s[pl.ds(r0, LIN_CHUNK), pcol], 0)
            kdm = jnp.where(keep, kd_s[pl.ds(r0, LIN_CHUNK), pcol], 0)
            ke = ke_s[pl.ds(r0, LIN_CHUNK), pcol]
            vh = vb_s[pl.ds(r0, LIN_CHUNK), h * GLA_DV:(h + 1) * GLA_DV]
            st = st_ref[h]
            attn = jnp.where(causal, _dot_nt(qm, ke), 0.0)
            o = _dot(attn.astype(BF16), vh) + _dot_nt(qm, st.astype(BF16))
            st_ref[h] = st * ecl_s[pl.ds(e0, 1), pcol] + _dot_tn(vh, kdm)
            o = o * lax.rsqrt(jnp.mean(o * o, -1, keepdims=True) + RMS_EPS) * ng
            gate = r_ref[pl.ds(r0, LIN_CHUNK), h * GLA_DV:(h + 1) * GLA_DV]
            o_ref[pl.ds(r0, LIN_CHUNK), h * GLA_DV:(h + 1) * GLA_DV] = (o * _silu(gate)).astype(o_ref.dtype)
        return carry

    lax.fori_loop(0, nc, chunk_body, 0)


def _gla(proj, glr, w_gate2, b_gate, norm_g, tt):
    s = proj.shape[0]
    hk = GLA_HEADS * GLA_DK
    hv = GLA_HEADS * GLA_DV
    wg2 = jnp.zeros((LANES, hk), F32).at[:GLA_GATE_RANK].set(w_gate2)
    return pl.pallas_call(
        functools.partial(_gla_kernel, tt=tt),
        grid=(s // tt,),
        in_specs=[pl.BlockSpec((tt, hk), lambda i: (i, 0)),
                  pl.BlockSpec((tt, hk), lambda i: (i, 1)),
                  pl.BlockSpec((tt, hv), lambda i: (i, 1)),
                  pl.BlockSpec((tt, hv), lambda i: (i, 2)),
                  pl.BlockSpec((tt, LANES), lambda i: (i, 0)),
                  pl.BlockSpec((LANES, hk), lambda i: (0, 0)),
                  pl.BlockSpec((1, hk), lambda i: (0, 0)),
                  pl.BlockSpec((1, GLA_DV), lambda i: (0, 0))],
        out_specs=pl.BlockSpec((tt, hv), lambda i: (i, 0)),
        out_shape=jax.ShapeDtypeStruct((s, hv), BF16),
        scratch_shapes=[pltpu.VMEM((GLA_HEADS, GLA_DV, LANES), F32),
                        pltpu.VMEM((tt, hk), BF16), pltpu.VMEM((tt, hk), BF16), pltpu.VMEM((tt, hk), BF16),
                        pltpu.VMEM((tt, hv), BF16),
                        pltpu.VMEM((tt // LIN_CHUNK * SUBLANES, hk), F32)],
        compiler_params=_params(("arbitrary",)),
        name="gla",
    )(proj, proj, proj, proj, glr, wg2, b_gate.reshape(1, hk), norm_g.reshape(1, GLA_DV))


def _moba_kernel(q_ref, k_ref, v_ref, slope_ref, o_ref, kmean_s, base_s, *, n_blk):
    blk = MOBA_BLOCK
    hd = MOBA_HEAD_DIM
    i = pl.program_id(1)
    slope = slope_ref[...][:, 0:1]

    @pl.when(i == 0)
    def _():
        kmean_s[...] = jnp.zeros_like(kmean_s)
        for n in range(n_blk):
            kmean_s[n:n + 1, :] = jnp.mean(k_ref[n * blk:(n + 1) * blk, :], axis=0, keepdims=True)
        r = lax.broadcasted_iota(jnp.int32, (blk, blk), 0)
        c = lax.broadcasted_iota(jnp.int32, (blk, blk), 1)
        base_s[...] = -slope * (r - c).astype(F32)

    q = q_ref[...]
    lane = lax.broadcasted_iota(jnp.int32, (blk, LANES), 1)
    lane_f = lane.astype(F32)
    gate = _dot_nt(q, kmean_s[...], HIGHEST)
    gate = jnp.where(lane < i, gate, -jnp.inf)
    chosen = jnp.zeros((blk, LANES), jnp.bool_)
    for _ in range(MOBA_TOPK):
        best = jnp.max(gate, axis=-1, keepdims=True)
        idx = jnp.min(jnp.where(gate == best, lane_f, float(LANES)), axis=-1, keepdims=True)
        hit = lane_f == idx
        chosen = jnp.logical_or(chosen, hit)
        gate = jnp.where(hit, -jnp.inf, gate)
    sel_bias = jnp.where(jnp.logical_and(chosen, lane < i), 0.0, NEG_BIG)
    qs = (q * (hd ** -0.5)).astype(BF16)
    q_ext = jnp.concatenate([qs, sel_bias.astype(BF16)], axis=1)

    r = lax.broadcasted_iota(jnp.int32, (blk, blk), 0)
    c = lax.broadcasted_iota(jnp.int32, (blk, blk), 1)
    own0 = pl.multiple_of(i * blk, blk)
    s = _dot_nt(qs, k_ref[pl.ds(own0, blk), :].astype(BF16)) + base_s[...]
    s = jnp.where(r >= c, s, -jnp.inf)
    m0 = jnp.max(s, axis=-1, keepdims=True)
    p = jnp.exp(s - m0)
    l0 = jnp.sum(p, axis=-1, keepdims=True)
    acc0 = _dot(p.astype(BF16), v_ref[pl.ds(own0, blk), :].astype(BF16))

    def body(n, carry):
        m, l, acc = carry
        k0 = pl.multiple_of(n * blk, blk)
        kn = k_ref[pl.ds(k0, blk), :].astype(BF16)
        onehot = jnp.where(lane == n, 1.0, 0.0).astype(BF16)
        k_ext = jnp.concatenate([kn, onehot], axis=1)
        dist0 = ((i - n) * blk).astype(F32)
        sc = _dot_nt(q_ext, k_ext) + (base_s[...] - slope * dist0)
        m_new = jnp.maximum(m, jnp.max(sc, axis=-1, keepdims=True))
        a = jnp.exp(m - m_new)
        pn = jnp.exp(sc - m_new)
        l_new = a * l + jnp.sum(pn, axis=-1, keepdims=True)
        acc_new = a * acc + _dot(pn.astype(BF16), v_ref[pl.ds(k0, blk), :].astype(BF16))
        return m_new, l_new, acc_new

    m, l, acc = lax.fori_loop(0, i, body, (m0, l0, acc0))
    o_ref[...] = (acc / l).astype(o_ref.dtype)


def _moba(proj, col0):
    s = proj.shape[0]
    n_blk = s // MOBA_BLOCK
    slopes = jnp.exp2(-8.0 * jnp.arange(1, MOBA_HEADS + 1, dtype=F32) / MOBA_HEADS)
    slopes = jnp.broadcast_to(slopes[:, None, None], (MOBA_HEADS, 1, LANES))
    return pl.pallas_call(
        functools.partial(_moba_kernel, n_blk=n_blk),
        grid=(MOBA_HEADS, n_blk),
        in_specs=[pl.BlockSpec((MOBA_BLOCK, MOBA_HEAD_DIM), lambda h, i: (i, col0 + h)),
                  pl.BlockSpec((s, MOBA_HEAD_DIM), lambda h, i: (0, col0 + MOBA_HEADS + h)),
                  pl.BlockSpec((s, MOBA_HEAD_DIM), lambda h, i: (0, col0 + 2 * MOBA_HEADS + h)),
                  pl.BlockSpec((None, 1, LANES), lambda h, i: (h, 0, 0))],
        out_specs=pl.BlockSpec((MOBA_BLOCK, MOBA_HEAD_DIM), lambda h, i: (i, h)),
        out_shape=jax.ShapeDtypeStruct((s, MOBA_HEADS * MOBA_HEAD_DIM), BF16),
        scratch_shapes=[pltpu.VMEM((LANES, MOBA_HEAD_DIM), F32), pltpu.VMEM((MOBA_BLOCK, MOBA_BLOCK), F32)],
        compiler_params=_params(("parallel", "arbitrary")),
        name="moba",
    )(proj, proj, proj, slopes)


def _unit_lower_inverse(low):
    n = low.shape[0]
    r = lax.broadcasted_iota(jnp.int32, (n, n), 0)
    c = lax.broadcasted_iota(jnp.int32, (n, n), 1)
    eye = jnp.where(r == c, 1.0, 0.0)
    same16 = (r // 16) == (c // 16)
    same32 = (r // 32) == (c // 32)
    ld = jnp.where(same16, low, 0.0)
    p2 = _dot(ld, ld, HIGHEST)
    p4 = _dot(p2, p2, HIGHEST)
    p8 = _dot(p4, p4, HIGHEST)
    t = _dot(eye - ld, eye + p2, HIGHEST)
    t = _dot(t, eye + p4, HIGHEST)
    t = _dot(t, eye + p8, HIGHEST)
    c32 = jnp.where(jnp.logical_and(same32, jnp.logical_not(same16)), low, 0.0)
    t = t - _dot(_dot(t, c32, HIGHEST), t, HIGHEST)
    c64 = jnp.where(same32, 0.0, low)
    t = t - _dot(_dot(t, c64, HIGHEST), t, HIGHEST)
    return t


def _gdn_kernel(q_ref, k_ref, v_ref, z_ref, bd_ref, cwq_ref, cwk_ref, cwv_ref, alog_ref, dtb_ref, ng_ref,
                o_ref, st_ref, cq_s, ck_s, cv_s, qn_s, kn_s, vn_s, gc_s, bt_s,
                u_s, w_s, at_s, qd_s, kd_s, egl_s, *, tt):
    nc = tt // LIN_CHUNK
    hd = GDN_HEAD_DIM
    h = pl.program_id(0)

    @pl.when(pl.program_id(1) == 0)
    def _():
        st_ref[...] = jnp.zeros_like(st_ref)
        cq_s[...] = jnp.zeros_like(cq_s)
        ck_s[...] = jnp.zeros_like(ck_s)
        cv_s[...] = jnp.zeros_like(cv_s)

    def conv_silu(x_ref, cw_ref, carry, dst):
        x = x_ref[...]
        y, y_top = _causal_conv_rows(x, cw_ref[...], carry[...])
        carry[...] = x[tt - SUBLANES:]
        dst[...] = _silu(y)
        dst[0:SUBLANES, :] = _silu(y_top)

    conv_silu(q_ref, cwq_ref, cq_s, qn_s)
    conv_silu(k_ref, cwk_ref, ck_s, kn_s)
    conv_silu(v_ref, cwv_ref, cv_s, vn_s)
    q = qn_s[...]
    k = kn_s[...]
    qn_s[...] = q * lax.rsqrt(jnp.sum(q * q, -1, keepdims=True) + RMS_EPS) * (hd ** -0.5)
    kn_s[...] = k * lax.rsqrt(jnp.sum(k * k, -1, keepdims=True) + RMS_EPS)

    bd = bd_ref[...]
    lane = lax.broadcasted_iota(jnp.int32, bd.shape, 1)
    beta_in = jnp.sum(jnp.where(lane == h, bd, 0.0), -1, keepdims=True)
    dec_in = jnp.sum(jnp.where(lane == GDN_HEADS + h, bd, 0.0), -1, keepdims=True)
    a_log = alog_ref[...][:, 0:1]
    dt_bias = dtb_ref[...][:, 0:1]
    xs = dec_in + dt_bias
    softplus = jnp.maximum(xs, 0.0) + jnp.log(1.0 + jnp.exp(-jnp.abs(xs)))
    g = -jnp.exp(a_log) * softplus
    gc_s[...] = _chunk_cumsum(jnp.broadcast_to(g, (tt, LANES)), LIN_CHUNK)
    bt_s[...] = jnp.broadcast_to(_sigmoid(beta_in), (tt, LANES))

    ri = lax.broadcasted_iota(jnp.int32, (LIN_CHUNK, LIN_CHUNK), 0)
    ci = lax.broadcasted_iota(jnp.int32, (LIN_CHUNK, LIN_CHUNK), 1)
    causal = ri >= ci
    strict = ri > ci

    def local_body(c, carry):
        r0 = pl.multiple_of(c * LIN_CHUNK, LIN_CHUNK)
        rows = pl.ds(r0, LIN_CHUNK)
        qc = qn_s[rows, :]
        kc = kn_s[rows, :]
        vc = vn_s[rows, :]
        gc = gc_s[rows, :]
        beta = bt_s[rows, :]
        gc_row = gc.T[0:LIN_CHUNK, :]
        diff = gc[:, 0:LIN_CHUNK] - gc_row
        decay = jnp.where(causal, jnp.exp(jnp.where(causal, diff, 0.0)), 0.0)
        kb = kc * beta
        kcb = kc.astype(BF16)
        low = jnp.where(strict, _dot_nt(kb.astype(BF16), kcb) * decay, 0.0)
        tinv = _unit_lower_inverse(low)
        egc = jnp.exp(gc)
        u_s[rows, :] = _dot(tinv, vc * beta, HIGHEST)
        w_s[rows, :] = _dot(tinv, kb * egc, HIGHEST)
        at_s[rows, :] = (_dot_nt(qc.astype(BF16), kcb) * decay).astype(BF16)
        qd_s[rows, :] = (qc * egc).astype(BF16)
        g_last = gc[LIN_CHUNK - 1:LIN_CHUNK, :]
        kd_s[rows, :] = (kc * jnp.exp(g_last - gc)).astype(BF16)
        e0 = pl.multiple_of(c * SUBLANES, SUBLANES)
        egl_s[pl.ds(e0, SUBLANES), :] = jnp.broadcast_to(jnp.exp(g_last), (SUBLANES, LANES))
        return carry

    lax.fori_loop(0, nc, local_body, 0)

    ng = ng_ref[...]

    def scan_body(c, carry):
        r0 = pl.multiple_of(c * LIN_CHUNK, LIN_CHUNK)
        rows = pl.ds(r0, LIN_CHUNK)
        e0 = pl.multiple_of(c * SUBLANES, SUBLANES)
        st = st_ref[...]
        stb = st.astype(BF16)
        v_new = u_s[rows, :] - _dot(w_s[rows, :].astype(BF16), stb)
        vnb = v_new.astype(BF16)
        o = _dot(qd_s[rows, :], stb) + _dot(at_s[rows, :], vnb)
        st_ref[...] = st * egl_s[pl.ds(e0, 1), :] + _dot_tn(kd_s[rows, :], vnb)
        o = o * lax.rsqrt(jnp.mean(o * o, -1, keepdims=True) + RMS_EPS) * ng
        o_ref[rows, :] = (o * _silu(z_ref[rows, :])).astype(o_ref.dtype)
        return carry

    lax.fori_loop(0, nc, scan_body, 0)


def _gdn(proj, bd, conv_w, a_log, dt_bias, norm_g, tt):
    s = proj.shape[0]
    nh = GDN_HEADS
    hd = GDN_HEAD_DIM
    per_head = lambda a: jnp.broadcast_to(a.astype(F32)[:, None, None], (nh, 1, LANES))
    blk = lambda off: pl.BlockSpec((tt, hd), lambda h, i: (i, off + h))
    cw = lambda off: pl.BlockSpec((GDN_CONV, hd), lambda h, i: (0, off + h))
    scal = pl.BlockSpec((None, 1, LANES), lambda h, i: (h, 0, 0))
    nce = tt // LIN_CHUNK * SUBLANES
    return pl.pallas_call(
        functools.partial(_gdn_kernel, tt=tt),
        grid=(nh, s // tt),
        in_specs=[blk(0), blk(nh), blk(2 * nh), blk(3 * nh),
                  pl.BlockSpec((tt, LANES), lambda h, i: (i, 0)),
                  cw(0), cw(nh), cw(2 * nh), scal, scal,
                  pl.BlockSpec((1, hd), lambda h, i: (0, 0))],
        out_specs=pl.BlockSpec((tt, hd), lambda h, i: (i, h)),
        out_shape=jax.ShapeDtypeStruct((s, nh * hd), BF16),
        scratch_shapes=[pltpu.VMEM((hd, hd), F32),
                        pltpu.VMEM((SUBLANES, hd), F32), pltpu.VMEM((SUBLANES, hd), F32),
                        pltpu.VMEM((SUBLANES, hd), F32),
                        pltpu.VMEM((tt, hd), F32), pltpu.VMEM((tt, hd), F32), pltpu.VMEM((tt, hd), F32),
                        pltpu.VMEM((tt, LANES), F32), pltpu.VMEM((tt, LANES), F32),
                        pltpu.VMEM((tt, hd), F32), pltpu.VMEM((tt, hd), F32),
                        pltpu.VMEM((tt, LIN_CHUNK), BF16),
                        pltpu.VMEM((tt, hd), BF16), pltpu.VMEM((tt, hd), BF16),
                        pltpu.VMEM((nce, LANES), F32)],
        compiler_params=_params(("parallel", "arbitrary")),
        name="gdn",
    )(proj, proj, proj, proj, bd, conv_w, conv_w, conv_w, per_head(a_log), per_head(dt_bias),
      norm_g.reshape(1, hd))


S5_LANE_CHUNK = 512
S5_GROUPS_PER_TILE = LANES // S5_GROUP_CH


def _s5_kernel(u_ref, bre_ref, bim_ref, cre_ref, cim_ref, are_ref, aim_ref, pre_ref, pim_ref, d_ref,
               o_ref, nat_s, up_s, xr_s, xi_s, cr_s, ci_s, gr_s, gi_s, *, tt):
    nseg = SUBLANES
    seg = tt // nseg
    nstate = S5_GROUPS * S5_STATE
    ntile = S5_CHANNELS // LANES
    spt = S5_GROUPS_PER_TILE * S5_STATE

    @pl.when(pl.program_id(0) == 0)
    def _():
        gr_s[...] = jnp.zeros_like(gr_s)
        gi_s[...] = jnp.zeros_like(gi_s)

    for t in range(ntile):
        nat_s[t] = u_ref[:, t * LANES:(t + 1) * LANES]
    for t in range(ntile):
        for j in range(seg):
            up_s[t, j * nseg:(j + 1) * nseg, :] = nat_s[t, pl.ds(j, nseg, stride=seg), :]
    for t in range(ntile):
        ub = up_s[t].astype(BF16)
        xr_s[:, t * spt:(t + 1) * spt] = _dot(ub, bre_ref[t])
        xi_s[:, t * spt:(t + 1) * spt] = _dot(ub, bim_ref[t])

    for lc in range(nstate // S5_LANE_CHUNK):
        cols = slice(lc * S5_LANE_CHUNK, (lc + 1) * S5_LANE_CHUNK)
        ar = are_ref[:, cols]
        ai = aim_ref[:, cols]

        def local_step(j, carry):
            xr, xi = carry
            r0 = pl.multiple_of(j * nseg, nseg)
            nr = ar * xr - ai * xi + xr_s[pl.ds(r0, nseg), cols]
            ni = ar * xi + ai * xr + xi_s[pl.ds(r0, nseg), cols]
            xr_s[pl.ds(r0, nseg), cols] = nr
            xi_s[pl.ds(r0, nseg), cols] = ni
            return nr, ni

        zero = jnp.zeros((nseg, S5_LANE_CHUNK), F32)
        er, ei = lax.fori_loop(0, seg, local_step, (zero, zero))
        pr = pre_ref[seg - 1:seg, cols]
        pi = pim_ref[seg - 1:seg, cols]
        cr = gr_s[:, cols]
        ci = gi_s[:, cols]
        for r in range(nseg):
            cr_s[r:r + 1, cols] = cr
            ci_s[r:r + 1, cols] = ci
            cr, ci = (pr * cr - pi * ci + er[r:r + 1], pr * ci + pi * cr + ei[r:r + 1])
        gr_s[:, cols] = cr
        gi_s[:, cols] = ci
        c_re = cr_s[:, cols]
        c_im = ci_s[:, cols]

        def fix_step(j, carry):
            r0 = pl.multiple_of(j * nseg, nseg)
            pjr = pre_ref[pl.ds(j, 1), cols]
            pji = pim_ref[pl.ds(j, 1), cols]
            xr_s[pl.ds(r0, nseg), cols] = xr_s[pl.ds(r0, nseg), cols] + pjr * c_re - pji * c_im
            xi_s[pl.ds(r0, nseg), cols] = xi_s[pl.ds(r0, nseg), cols] + pjr * c_im + pji * c_re
            return carry

        lax.fori_loop(0, seg, fix_step, 0)

    for t in range(ntile):
        xr = xr_s[:, t * spt:(t + 1) * spt].astype(BF16)
        xi = xi_s[:, t * spt:(t + 1) * spt].astype(BF16)
        y = _dot(xr, cre_ref[t]) - _dot(xi, cim_ref[t]) + d_ref[:, t * LANES:(t + 1) * LANES] * up_s[t]
        up_s[t] = 0.5 * y * (1.0 + jnp.tanh(math.sqrt(2.0 / math.pi) * (y + 0.044715 * (y * y * y))))
    for t in range(ntile):
        for j in range(seg):
            nat_s[t, pl.ds(j, nseg, stride=seg), :] = up_s[t, j * nseg:(j + 1) * nseg, :]
    for t in range(ntile):
        o_ref[:, t * LANES:(t + 1) * LANES] = nat_s[t]


def _s5(proj, col0, a_re, a_im, b_re, b_im, c_re, c_im, d, log_step, tt):
    s = proj.shape[0]
    seg = tt // SUBLANES
    nstate = S5_GROUPS * S5_STATE
    ntile = S5_CHANNELS // LANES
    gpt = S5_GROUPS_PER_TILE
    a_re, a_im, b_re, b_im, c_re, c_im, d = (t.astype(F32) for t in (a_re, a_im, b_re, b_im, c_re, c_im, d))
    step = jnp.exp(log_step.astype(F32))[:, None]
    mag = jnp.exp(a_re * step)
    ab_re = mag * jnp.cos(a_im * step)
    ab_im = mag * jnp.sin(a_im * step)
    den = jnp.square(a_re) + jnp.square(a_im)
    z_re = ((ab_re - 1.0) * a_re + ab_im * a_im) / den
    z_im = (ab_im * a_re - (ab_re - 1.0) * a_im) / den
    bb_re = z_re[..., None] * b_re - z_im[..., None] * b_im
    bb_im = z_re[..., None] * b_im + z_im[..., None] * b_re

    def block_diag_in(bb):
        bt = bb.reshape(ntile, gpt, S5_STATE, S5_GROUP_CH)
        eye = jnp.eye(gpt, dtype=F32)
        m = jnp.einsum('tgpc,gh->tgchp', bt, eye)
        return m.reshape(ntile, gpt * S5_GROUP_CH, gpt * S5_STATE).astype(BF16)

    def block_diag_out(cc):
        ct = cc.reshape(ntile, gpt, S5_GROUP_CH, S5_STATE)
        eye = jnp.eye(gpt, dtype=F32)
        m = jnp.einsum('tgcp,gh->tgphc', ct, eye)
        return m.reshape(ntile, gpt * S5_STATE, gpt * S5_GROUP_CH).astype(BF16)

    def pow_step(carry, _):
        pr, pi = carry
        nr, ni = pr * ab_re - pi * ab_im, pr * ab_im + pi * ab_re
        return (nr, ni), (pr, pi)

    _, (pw_re, pw_im) = lax.scan(pow_step, (ab_re, ab_im), None, length=seg)
    pw_re = pw_re.reshape(seg, nstate)
    pw_im = pw_im.reshape(seg, nstate)
    full = lambda shape: pl.BlockSpec(shape, lambda i: (0,) * len(shape))
    return pl.pallas_call(
        functools.partial(_s5_kernel, tt=tt),
        grid=(s // tt,),
        in_specs=[pl.BlockSpec((tt, S5_CHANNELS), lambda i: (i, col0)),
                  full((ntile, LANES, gpt * S5_STATE)), full((ntile, LANES, gpt * S5_STATE)),
                  full((ntile, gpt * S5_STATE, LANES)), full((ntile, gpt * S5_STATE, LANES)),
                  full((1, nstate)), full((1, nstate)),
                  full((seg, nstate)), full((seg, nstate)),
                  full((1, S5_CHANNELS))],
        out_specs=pl.BlockSpec((tt, S5_CHANNELS), lambda i: (i, 0)),
        out_shape=jax.ShapeDtypeStruct((s, S5_CHANNELS), F32),
        scratch_shapes=[pltpu.VMEM((ntile, tt, LANES), F32), pltpu.VMEM((ntile, tt, LANES), F32),
                        pltpu.VMEM((tt, nstate), F32), pltpu.VMEM((tt, nstate), F32),
                        pltpu.VMEM((SUBLANES, nstate), F32), pltpu.VMEM((SUBLANES, nstate), F32),
                        pltpu.VMEM((1, nstate), F32), pltpu.VMEM((1, nstate), F32)],
        compiler_params=_params(("arbitrary",)),
        name="s5",
    )(proj, block_diag_in(bb_re), block_diag_in(bb_im), block_diag_out(c_re), block_diag_out(c_im),
      ab_re.reshape(1, nstate), ab_im.reshape(1, nstate), pw_re, pw_im, d.reshape(1, S5_CHANNELS))


def _glu_kernel(y_ref, w_ref, b_ref, o_ref):
    y = y_ref[...]
    t = _dot(y.astype(BF16), w_ref[...]) + b_ref[...]
    o_ref[...] = (y * _sigmoid(t)).astype(o_ref.dtype)


def _glu(y, w, b, tm):
    m, n = y.shape
    return pl.pallas_call(
        _glu_kernel,
        grid=(m // tm,),
        in_specs=[pl.BlockSpec((tm, n), lambda i: (i, 0)),
                  pl.BlockSpec((n, n), lambda i: (0, 0)),
                  pl.BlockSpec((1, n), lambda i: (0, 0))],
        out_specs=pl.BlockSpec((tm, n), lambda i: (i, 0)),
        out_shape=jax.ShapeDtypeStruct((m, n), BF16),
        compiler_params=_params(("parallel",)),
        name="glu",
    )(y, w, b.reshape(1, n))


def _pad_cols(w, n):
    return jnp.concatenate([w, jnp.zeros((w.shape[0], n - w.shape[1]), w.dtype)], axis=1)


def _split_cols(w, sizes):
    offs = [0]
    for sz in sizes:
        offs.append(offs[-1] + sz)
    return [w[:, offs[i]:offs[i + 1]] for i in range(len(sizes))]


def _even_mixer(xb, w_in, w_gate2, b_gate, norm_g):
    hk, hv, hm = GLA_HEADS * GLA_DK, GLA_HEADS * GLA_DV, MOBA_HEADS * MOBA_HEAD_DIM
    gq, gk, gv, g_lr, gr, mq, mk, mv = _split_cols(w_in, (hk, hk, hv, GLA_GATE_RANK, hv, hm, hm, hm))
    w_main = jnp.concatenate([gq, gk, gv, gr, mq, mk, mv], axis=1).astype(BF16)
    proj = _matmul(xb, w_main, F32, 1024, 1024)
    glr = _matmul(xb, _pad_cols(g_lr, LANES).astype(BF16), F32, 1024, LANES)
    gla_out = _gla(proj, glr, w_gate2, b_gate, norm_g, 512)
    moba_out = _moba(proj, (2 * hk + 2 * hv) // LANES)
    return jnp.concatenate([gla_out, moba_out], axis=-1)


def _odd_mixer(xb, w_in, conv_w, a_log, dt_bias, norm_g, a_re, a_im, b_re, b_im, c_re, c_im, d, log_step,
               glu_w, glu_b):
    hq = GDN_HEADS * GDN_HEAD_DIM
    qkv, z, beta, decay, u = _split_cols(w_in, (3 * hq, hq, GDN_HEADS, GDN_HEADS, S5_CHANNELS))
    w_main = jnp.concatenate([qkv, z, u], axis=1).astype(BF16)
    proj = _matmul(xb, w_main, F32, 1024, 1024)
    bd = _matmul(xb, _pad_cols(jnp.concatenate([beta, decay], axis=1), LANES).astype(BF16), F32, 1024, LANES)
    gdn_out = _gdn(proj, bd, conv_w, a_log, dt_bias, norm_g, 512)
    y = _s5(proj, 4 * hq // S5_CHANNELS, a_re, a_im, b_re, b_im, c_re, c_im, d, log_step, 512)
    s5_out = _glu(y, glu_w.astype(BF16), glu_b, 512)
    return jnp.concatenate([gdn_out, s5_out], axis=-1)


def kernel(x, even_w_in, gla_w_gate2, gla_b_gate, gla_norm_g, even_w_out, odd_w_in, gdn_conv_w, gdn_a_log,
           gdn_dt_bias, gdn_norm_g, s5_a_re, s5_a_im, s5_b_re, s5_b_im, s5_c_re, s5_c_im, s5_d, s5_log_step,
           s5_glu_w, s5_glu_b, odd_w_out, ln_mix_g, ln_mix_b, ffn_w_up, ffn_conv_w, ffn_w_down, ln_ffn_g, ln_ffn_b):
    bsz, seq, dm = x.shape
    outs = []
    for b in range(bsz):
        xf = x[b]
        xb = xf.astype(BF16)
        for i in range(DEPTH):
            j = i // 2
            if i % 2 == 0:
                mix_in = _even_mixer(xb, even_w_in[j], gla_w_gate2[j], gla_b_gate[j], gla_norm_g[j])
                w_out = even_w_out[j]
            else:
                mix_in = _odd_mixer(xb, odd_w_in[j], gdn_conv_w[j], gdn_a_log[j], gdn_dt_bias[j], gdn_norm_g[j],
                                    s5_a_re[j], s5_a_im[j], s5_b_re[j], s5_b_im[j], s5_c_re[j], s5_c_im[j],
                                    s5_d[j], s5_log_step[j], s5_glu_w[j], s5_glu_b[j])
                w_out = odd_w_out[j]
            xf, xb = _matmul_res_ln(mix_in, w_out.astype(BF16), xf, ln_mix_g[i], ln_mix_b[i], 512, 512)
            mid = _ffn_up(xb, ffn_w_up[i].astype(BF16), ffn_conv_w[i], 1024, 512)
            xf, xb = _matmul_res_ln(mid, ffn_w_down[i].astype(BF16), xf, ln_ffn_g[i], ln_ffn_b[i], 512, 512)
        outs.append(xf)
    return jnp.stack(outs, axis=0)
```

```python
import functools
import math

import jax
import jax.numpy as jnp
from jax import lax
from jax.experimental import pallas as pl
from jax.experimental.pallas import tpu as pltpu

F32 = jnp.float32
BF16 = jnp.bfloat16
HIGHEST = lax.Precision.HIGHEST

DEPTH = 4
GLA_HEADS, GLA_DK, GLA_DV, GLA_GATE_RANK, GLA_GATE_NORM = 8, 64, 128, 16, 16.0
MOBA_HEADS, MOBA_HEAD_DIM, MOBA_BLOCK, MOBA_TOPK = 8, 128, 256, 3
GDN_HEADS, GDN_HEAD_DIM, GDN_CONV = 8, 128, 4
S5_CHANNELS, S5_GROUP_CH, S5_STATE = 1024, 16, 64
S5_GROUPS = S5_CHANNELS // S5_GROUP_CH
LIN_CHUNK = 64
D_FF = 5632
FFN_CONV = 3
DEEPNORM_ALPHA = (2 * DEPTH) ** 0.25
LN_EPS = 1e-5
RMS_EPS = 1e-6

LANES = 128
SUBLANES = 8
VMEM_LIMIT_BYTES = 56 * 1024 * 1024
NEG_BIG = -1e30


def _params(semantics):
    return pltpu.CompilerParams(dimension_semantics=semantics, vmem_limit_bytes=VMEM_LIMIT_BYTES)


def _dot(a, b, precision=None):
    return jnp.dot(a, b, preferred_element_type=F32, precision=precision)


def _dot_nt(a, b, precision=None):
    return lax.dot_general(a, b, (((1,), (1,)), ((), ())), preferred_element_type=F32, precision=precision)


def _dot_tn(a, b, precision=None):
    return lax.dot_general(a, b, (((0,), (0,)), ((), ())), preferred_element_type=F32, precision=precision)


def _sigmoid(x):
    return 1.0 / (1.0 + jnp.exp(-x))


def _silu(x):
    return x * _sigmoid(x)


def _chunk_cumsum(x, chunk):
    pos = lax.broadcasted_iota(jnp.int32, x.shape, 0) % chunk
    k = 1
    while k < chunk:
        x = x + jnp.where(pos >= k, pltpu.roll(x, k, 0), 0.0)
        k *= 2
    return x


def _mm_kernel(a_ref, b_ref, o_ref):
    o_ref[...] = _dot(a_ref[...].astype(BF16), b_ref[...]).astype(o_ref.dtype)


def _matmul(a, b, out_dtype, tm, tn):
    m, k = a.shape
    n = b.shape[1]
    return pl.pallas_call(
        _mm_kernel,
        grid=(m // tm, n // tn),
        in_specs=[pl.BlockSpec((tm, k), lambda i, j: (i, 0)),
                  pl.BlockSpec((k, tn), lambda i, j: (0, j))],
        out_specs=pl.BlockSpec((tm, tn), lambda i, j: (i, j)),
        out_shape=jax.ShapeDtypeStruct((m, n), out_dtype),
        compiler_params=_params(("parallel", "parallel")),
        name="matmul",
    )(a, b)


def _mm_ln_kernel(a_ref, b_ref, x_ref, g_ref, bt_ref, of_ref, ob_ref, acc_ref, *, nk):
    k = pl.program_id(1)

    @pl.when(k == 0)
    def _():
        acc_ref[...] = jnp.zeros_like(acc_ref)

    acc_ref[...] += _dot(a_ref[...], b_ref[...])

    @pl.when(k == nk - 1)
    def _():
        y = DEEPNORM_ALPHA * x_ref[...] + acc_ref[...]
        mu = jnp.mean(y, -1, keepdims=True)
        d = y - mu
        var = jnp.mean(d * d, -1, keepdims=True)
        o = d * lax.rsqrt(var + LN_EPS) * g_ref[...] + bt_ref[...]
        of_ref[...] = o
        ob_ref[...] = o.astype(BF16)


def _matmul_res_ln(a, b, x, g, bt, tm, tk):
    m, k = a.shape
    n = b.shape[1]
    nk = k // tk
    return pl.pallas_call(
        functools.partial(_mm_ln_kernel, nk=nk),
        grid=(m // tm, nk),
        in_specs=[pl.BlockSpec((tm, tk), lambda i, kk: (i, kk)),
                  pl.BlockSpec((tk, n), lambda i, kk: (kk, 0)),
                  pl.BlockSpec((tm, n), lambda i, kk: (i, 0)),
                  pl.BlockSpec((1, n), lambda i, kk: (0, 0)),
                  pl.BlockSpec((1, n), lambda i, kk: (0, 0))],
        out_specs=[pl.BlockSpec((tm, n), lambda i, kk: (i, 0)),
                   pl.BlockSpec((tm, n), lambda i, kk: (i, 0))],
        out_shape=[jax.ShapeDtypeStruct((m, n), F32), jax.ShapeDtypeStruct((m, n), BF16)],
        scratch_shapes=[pltpu.VMEM((tm, n), F32)],
        compiler_params=_params(("parallel", "arbitrary")),
        name="matmul_res_ln",
    )(a, b, x, g.reshape(1, n), bt.reshape(1, n))


def _causal_conv_rows(h, w, prev):
    width = w.shape[0]
    out = w[width - 1:width] * h
    top = jnp.concatenate([prev, h[0:SUBLANES]], axis=0)
    out_top = w[width - 1:width] * top
    for s in range(1, width):
        wj = w[width - 1 - s:width - s]
        out = out + wj * pltpu.roll(h, s, 0)
        out_top = out_top + wj * pltpu.roll(top, s, 0)
    return out, out_top[SUBLANES:]


def _ffn_up_kernel(x_ref, wg_ref, wv_ref, cg_ref, cv_ref, o_ref, carry_g, carry_v):
    @pl.when(pl.program_id(1) == 0)
    def _():
        carry_g[...] = jnp.zeros_like(carry_g)
        carry_v[...] = jnp.zeros_like(carry_v)

    x = x_ref[...]
    tm = x.shape[0]
    hg = _dot(x, wg_ref[...])
    hv = _dot(x, wv_ref[...])
    g, g_top = _causal_conv_rows(hg, cg_ref[...], carry_g[...])
    v, v_top = _causal_conv_rows(hv, cv_ref[...], carry_v[...])
    carry_g[...] = hg[tm - SUBLANES:]
    carry_v[...] = hv[tm - SUBLANES:]
    o_ref[...] = (_silu(g) * v).astype(o_ref.dtype)
    o_ref[0:SUBLANES, :] = (_silu(g_top) * v_top).astype(o_ref.dtype)


def _ffn_up(xb, w_up, conv_w, tm, tn):
    m, k = xb.shape
    d_ff = w_up.shape[1] // 2
    nt = d_ff // tn
    return pl.pallas_call(
        _ffn_up_kernel,
        grid=(nt, m // tm),
        in_specs=[pl.BlockSpec((tm, k), lambda j, i: (i, 0)),
                  pl.BlockSpec((k, tn), lambda j, i: (0, j)),
                  pl.BlockSpec((k, tn), lambda j, i: (0, j + nt)),
                  pl.BlockSpec((FFN_CONV, tn), lambda j, i: (0, j)),
                  pl.BlockSpec((FFN_CONV, tn), lambda j, i: (0, j + nt))],
        out_specs=pl.BlockSpec((tm, tn), lambda j, i: (i, j)),
        out_shape=jax.ShapeDtypeStruct((m, d_ff), BF16),
        scratch_shapes=[pltpu.VMEM((SUBLANES, tn), F32), pltpu.VMEM((SUBLANES, tn), F32)],
        compiler_params=_params(("parallel", "arbitrary")),
        name="ffn_up",
    )(xb, w_up, w_up, conv_w, conv_w)


def _gla_kernel(q_ref, k_ref, v_ref, r_ref, glr_ref, wg2_ref, bg_ref, ng_ref, o_ref,
                st_ref, qe_s, ke_s, kd_s, vb_s, ecl_s, *, tt):
    nc = tt // LIN_CHUNK
    hk = GLA_HEADS * GLA_DK

    @pl.when(pl.program_id(0) == 0)
    def _():
        st_ref[...] = jnp.zeros_like(st_ref)

    z = _dot(glr_ref[...], wg2_ref[...], HIGHEST) + bg_ref[...]
    log_a = (jnp.minimum(z, 0.0) - jnp.log(1.0 + jnp.exp(-jnp.abs(z)))) / GLA_GATE_NORM
    cum = _chunk_cumsum(log_a, LIN_CHUNK)
    q = q_ref[...] * (GLA_DK ** -0.5)
    k = k_ref[...]
    qe_s[...] = (q * jnp.exp(cum)).astype(BF16)
    ke_s[...] = (k * jnp.exp(-cum)).astype(BF16)
    for c in range(nc):
        rows = slice(c * LIN_CHUNK, (c + 1) * LIN_CHUNK)
        cl = cum[(c + 1) * LIN_CHUNK - 1:(c + 1) * LIN_CHUNK, :]
        kd_s[rows, :] = (k[rows] * jnp.exp(cl - cum[rows])).astype(BF16)
        ecl_s[c * SUBLANES:(c + 1) * SUBLANES, :] = jnp.broadcast_to(jnp.exp(cl), (SUBLANES, hk))
    vb_s[...] = v_ref[...].astype(BF16)

    ri = lax.broadcasted_iota(jnp.int32, (LIN_CHUNK, LIN_CHUNK), 0)
    ci = lax.broadcasted_iota(jnp.int32, (LIN_CHUNK, LIN_CHUNK), 1)
    causal = ri >= ci
    lane = lax.broadcasted_iota(jnp.int32, (LIN_CHUNK, LANES), 1)
    lo_half = lane < GLA_DK
    ng = ng_ref[...]

    def chunk_body(c, carry):
        r0 = pl.multiple_of(c * LIN_CHUNK, LIN_CHUNK)
        e0 = pl.multiple_of(c * SUBLANES, SUBLANES)
        for h in range(GLA_HEADS):
            pcol = slice((h // 2) * LANES, (h // 2 + 1) * LANES)
            keep = lo_half if h % 2 == 0 else jnp.logical_not(lo_half)
            qm = jnp.where(keep, qe_s[pl.ds(r0, LIN_CHUNK), pcol], 0)
            kdm = jnp.where(keep, kd_s[pl.ds(r0, LIN_CHUNK), pcol], 0)
            ke = ke_s[pl.ds(r0, LIN_CHUNK), pcol]
            vh = vb_s[pl.ds(r0, LIN_CHUNK), h * GLA_DV:(h + 1) * GLA_DV]
            st = st_ref[h]
            attn = jnp.where(causal, _dot_nt(qm, ke), 0.0)
            o = _dot(attn.astype(BF16), vh) + _dot_nt(qm, st.astype(BF16))
            st_ref[h] = st * ecl_s[pl.ds(e0, 1), pcol] + _dot_tn(vh, kdm)
            o = o * lax.rsqrt(jnp.mean(o * o, -1, keepdims=True) + RMS_EPS) * ng
            gate = r_ref[pl.ds(r0, LIN_CHUNK), h * GLA_DV:(h + 1) * GLA_DV]
            o_ref[pl.ds(r0, LIN_CHUNK), h * GLA_DV:(h + 1) * GLA_DV] = (o * _silu(gate)).astype(o_ref.dtype)
        return carry

    lax.fori_loop(0, nc, chunk_body, 0)


def _gla(proj, glr, w_gate2, b_gate, norm_g, tt):
    s = proj.shape[0]
    hk = GLA_HEADS * GLA_DK
    hv = GLA_HEADS * GLA_DV
    wg2 = jnp.zeros((LANES, hk), F32).at[:GLA_GATE_RANK].set(w_gate2)
    return pl.pallas_call(
        functools.partial(_gla_kernel, tt=tt),
        grid=(s // tt,),
        in_specs=[pl.BlockSpec((tt, hk), lambda i: (i, 0)),
                  pl.BlockSpec((tt, hk), lambda i: (i, 1)),
                  pl.BlockSpec((tt, hv), lambda i: (i, 1)),
                  pl.BlockSpec((tt, hv), lambda i: (i, 2)),
                  pl.BlockSpec((tt, LANES), lambda i: (i, 0)),
                  pl.BlockSpec((LANES, hk), lambda i: (0, 0)),
                  pl.BlockSpec((1, hk), lambda i: (0, 0)),
                  pl.BlockSpec((1, GLA_DV), lambda i: (0, 0))],
        out_specs=pl.BlockSpec((tt, hv), lambda i: (i, 0)),
        out_shape=jax.ShapeDtypeStruct((s, hv), BF16),
        scratch_shapes=[pltpu.VMEM((GLA_HEADS, GLA_DV, LANES), F32),
                        pltpu.VMEM((tt, hk), BF16), pltpu.VMEM((tt, hk), BF16), pltpu.VMEM((tt, hk), BF16),
                        pltpu.VMEM((tt, hv), BF16),
                        pltpu.VMEM((tt // LIN_CHUNK * SUBLANES, hk), F32)],
        compiler_params=_params(("arbitrary",)),
        name="gla",
    )(proj, proj, proj, proj, glr, wg2, b_gate.reshape(1, hk), norm_g.reshape(1, GLA_DV))


def _split3(x):
    hi = x.astype(BF16).astype(F32)
    rest = x - hi
    mid = rest.astype(BF16).astype(F32)
    lo = (rest - mid).astype(BF16).astype(F32)
    return hi, mid, lo


MOBA_SEL_LANES = 32
MOBA_ROW_LANE = 96
MOBA_COL_LANE = 99
LOG2E = 1.4426950408889634


def _moba_kernel(q_ref, k_ref, v_ref, slope_ref, o_ref, kmean_s, kx_s, vb_s, qx_s, s_s, mx_s, mb_s, ls_s, acc_s,
                 *, n_blk):
    blk = MOBA_BLOCK
    hd = MOBA_HEAD_DIM
    i = pl.program_id(1)
    sl2 = slope_ref[...][:, 0:1] * LOG2E
    s_hi, s_mid, s_lo = _split3(sl2)
    lane = lax.broadcasted_iota(jnp.int32, (blk, LANES), 1)
    lane_f = lane.astype(F32)
    row_f = lax.broadcasted_iota(jnp.int32, (blk, LANES), 0).astype(F32)

    @pl.when(i == 0)
    def _():
        kmean_s[...] = jnp.zeros_like(kmean_s)
        k_extra = jnp.where(lane == MOBA_ROW_LANE, -s_hi, jnp.where(lane == MOBA_ROW_LANE + 1, -s_mid,
                  jnp.where(lane == MOBA_ROW_LANE + 2, -s_lo,
                  jnp.where(jnp.logical_and(lane >= MOBA_COL_LANE, lane < MOBA_COL_LANE + 3), row_f, 0.0))))
        for n in range(n_blk):
            rows = slice(n * blk, (n + 1) * blk)
            kb = k_ref[rows, :]
            kmean_s[n:n + 1, :] = jnp.mean(kb, axis=0, keepdims=True)
            onehot = jnp.logical_or(lane == n, jnp.logical_or(lane == MOBA_SEL_LANES + n,
                                                               lane == 2 * MOBA_SEL_LANES + n))
            kx_s[rows, 0:hd] = kb.astype(BF16)
            kx_s[rows, hd:] = jnp.where(onehot, 1.0, k_extra).astype(BF16)
            vb_s[rows, :] = v_ref[rows, :].astype(BF16)

    q = q_ref[...]
    gate = _dot_nt(q, kmean_s[...], HIGHEST)
    gate = jnp.where(lane < i, gate, -jnp.inf)
    chosen = jnp.zeros((blk, LANES), jnp.bool_)
    for _ in range(MOBA_TOPK):
        best = jnp.max(gate, axis=-1, keepdims=True)
        idx = jnp.min(jnp.where(gate == best, lane_f, float(LANES)), axis=-1, keepdims=True)
        hit = lane_f == idx
        chosen = jnp.logical_or(chosen, hit)
        gate = jnp.where(hit, -jnp.inf, gate)
    sel = jnp.where(jnp.logical_and(chosen, lane < i), -sl2 * ((i - lane) * blk).astype(F32), NEG_BIG)
    sel = jnp.where(lane == i, 0.0, sel)
    sel = jnp.where(lane < MOBA_SEL_LANES, sel, 0.0)
    b_hi, b_mid, b_lo = _split3(sel)
    ext = b_hi + pltpu.roll(b_mid, MOBA_SEL_LANES, 1) + pltpu.roll(b_lo, 2 * MOBA_SEL_LANES, 1)
    ext = jnp.where(jnp.logical_and(lane >= MOBA_ROW_LANE, lane < MOBA_ROW_LANE + 3), row_f, ext)
    ext = jnp.where(lane == MOBA_COL_LANE, s_hi, jnp.where(lane == MOBA_COL_LANE + 1, s_mid,
          jnp.where(lane == MOBA_COL_LANE + 2, s_lo, ext)))
    qx_s[:, 0:hd] = (q * (hd ** -0.5 * LOG2E)).astype(BF16)
    qx_s[:, hd:] = ext.astype(BF16)

    def scores(n):
        k0 = pl.multiple_of(n * blk, blk)
        return _dot_nt(qx_s[...], kx_s[pl.ds(k0, blk), :])

    r = lax.broadcasted_iota(jnp.int32, (blk, blk), 0)
    c = lax.broadcasted_iota(jnp.int32, (blk, blk), 1)
    s_own = jnp.where(r >= c, scores(i), NEG_BIG)
    s_s[i] = s_own
    mx_s[...] = s_own

    def max_pair(j, carry):
        sa = scores(2 * j)
        sb = scores(2 * j + 1)
        s_s[2 * j] = sa
        s_s[2 * j + 1] = sb
        mx_s[...] = jnp.maximum(mx_s[...], jnp.maximum(sa, sb))
        return carry

    lax.fori_loop(0, i // 2, max_pair, 0)

    @pl.when(i % 2 == 1)
    def _():
        sa = scores(i - 1)
        s_s[i - 1] = sa
        mx_s[...] = jnp.maximum(mx_s[...], sa)

    mb_s[...] = jnp.broadcast_to(jnp.max(mx_s[...], axis=-1, keepdims=True), (blk, blk))

    def probs(n):
        k0 = pl.multiple_of(n * blk, blk)
        p = jnp.exp2(s_s[n] - mb_s[...])
        return p, _dot(p.astype(BF16), vb_s[pl.ds(k0, blk), :])

    p, pv = probs(i)
    ls_s[...] = p
    acc_s[...] = pv

    def sum_pair(j, carry):
        pa, va = probs(2 * j)
        pb, vb = probs(2 * j + 1)
        ls_s[...] += pa + pb
        acc_s[...] += va + vb
        return carry

    lax.fori_loop(0, i // 2, sum_pair, 0)

    @pl.when(i % 2 == 1)
    def _():
        pa, va = probs(i - 1)
        ls_s[...] += pa
        acc_s[...] += va

    o_ref[...] = (acc_s[...] / jnp.sum(ls_s[...], axis=-1, keepdims=True)).astype(o_ref.dtype)


def _moba(proj, col0):
    s = proj.shape[0]
    n_blk = s // MOBA_BLOCK
    slopes = jnp.exp2(-8.0 * jnp.arange(1, MOBA_HEADS + 1, dtype=F32) / MOBA_HEADS)
    slopes = jnp.broadcast_to(slopes[:, None, None], (MOBA_HEADS, 1, LANES))
    return pl.pallas_call(
        functools.partial(_moba_kernel, n_blk=n_blk),
        grid=(MOBA_HEADS, n_blk),
        in_specs=[pl.BlockSpec((MOBA_BLOCK, MOBA_HEAD_DIM), lambda h, i: (i, col0 + h)),
                  pl.BlockSpec((s, MOBA_HEAD_DIM), lambda h, i: (0, col0 + MOBA_HEADS + h)),
                  pl.BlockSpec((s, MOBA_HEAD_DIM), lambda h, i: (0, col0 + 2 * MOBA_HEADS + h)),
                  pl.BlockSpec((None, 1, LANES), lambda h, i: (h, 0, 0))],
        out_specs=pl.BlockSpec((MOBA_BLOCK, MOBA_HEAD_DIM), lambda h, i: (i, h)),
        out_shape=jax.ShapeDtypeStruct((s, MOBA_HEADS * MOBA_HEAD_DIM), BF16),
        scratch_shapes=[pltpu.VMEM((LANES, MOBA_HEAD_DIM), F32),
                        pltpu.VMEM((s, MOBA_HEAD_DIM + LANES), BF16), pltpu.VMEM((s, MOBA_HEAD_DIM), BF16),
                        pltpu.VMEM((MOBA_BLOCK, MOBA_HEAD_DIM + LANES), BF16),
                        pltpu.VMEM((n_blk, MOBA_BLOCK, MOBA_BLOCK), F32),
                        pltpu.VMEM((MOBA_BLOCK, MOBA_BLOCK), F32), pltpu.VMEM((MOBA_BLOCK, MOBA_BLOCK), F32),
                        pltpu.VMEM((MOBA_BLOCK, MOBA_BLOCK), F32), pltpu.VMEM((MOBA_BLOCK, MOBA_HEAD_DIM), F32)],
        compiler_params=_params(("parallel", "arbitrary")),
        name="moba",
    )(proj, proj, proj, slopes)


GDN_CHUNK_GROUP = 8


def _unit_lower_inverse_minus_eye(lows):
    n = lows[0].shape[0]
    r = lax.broadcasted_iota(jnp.int32, (n, n), 0)
    c = lax.broadcasted_iota(jnp.int32, (n, n), 1)
    eye = jnp.where(r == c, 1.0, 0.0)
    same16 = (r // 16) == (c // 16)
    same32 = (r // 32) == (c // 32)
    bdot = lambda a, b: [_dot(x.astype(BF16), y.astype(BF16)) for x, y in zip(a, b)]
    ld = [jnp.where(same16, m, 0.0) for m in lows]
    p2 = bdot(ld, ld)
    p4 = bdot(p2, p2)
    p8 = bdot(p4, p4)
    lp = bdot(ld, p2)
    x = [a - b - c_ for a, b, c_ in zip(p2, ld, lp)]
    xp = bdot(x, p4)
    x = [a + b + c_ for a, b, c_ in zip(x, p4, xp)]
    xp = bdot(x, p8)
    t = [eye + a + b + c_ for a, b, c_ in zip(x, p8, xp)]
    c32 = [jnp.where(jnp.logical_and(same32, jnp.logical_not(same16)), m, 0.0) for m in lows]
    t = [a - b for a, b in zip(t, bdot(bdot(t, c32), t))]
    c64 = [jnp.where(same32, 0.0, m) for m in lows]
    t = [a - b for a, b in zip(t, bdot(bdot(t, c64), t))]
    return [a - eye for a in t]


def _gdn_local_kernel(q_ref, k_ref, v_ref, bd_ref, cwq_ref, cwk_ref, cwv_ref, alog_ref, dtb_ref,
                      u_s, w_s, at_s, qd_s, kd_s, egl_s, cq_s, ck_s, cv_s, qn_s, kn_s, vn_s, gc_s, bt_s, *, tt):
    nc = tt // LIN_CHUNK
    hd = GDN_HEAD_DIM
    h = pl.program_id(0)

    @pl.when(pl.program_id(1) == 0)
    def _():
        cq_s[...] = jnp.zeros_like(cq_s)
        ck_s[...] = jnp.zeros_like(ck_s)
        cv_s[...] = jnp.zeros_like(cv_s)

    def conv_silu(x_ref, cw_ref, carry, dst):
        x = x_ref[...]
        y, y_top = _causal_conv_rows(x, cw_ref[...], carry[...])
        carry[...] = x[tt - SUBLANES:]
        dst[...] = _silu(y)
        dst[0:SUBLANES, :] = _silu(y_top)

    conv_silu(q_ref, cwq_ref, cq_s, qn_s)
    conv_silu(k_ref, cwk_ref, ck_s, kn_s)
    conv_silu(v_ref, cwv_ref, cv_s, vn_s)
    q = qn_s[...]
    k = kn_s[...]
    qn_s[...] = q * lax.rsqrt(jnp.sum(q * q, -1, keepdims=True) + RMS_EPS) * (hd ** -0.5)
    kn_s[...] = k * lax.rsqrt(jnp.sum(k * k, -1, keepdims=True) + RMS_EPS)

    bd = bd_ref[...]
    lane = lax.broadcasted_iota(jnp.int32, bd.shape, 1)
    beta_in = jnp.sum(jnp.where(lane == h, bd, 0.0), -1, keepdims=True)
    dec_in = jnp.sum(jnp.where(lane == GDN_HEADS + h, bd, 0.0), -1, keepdims=True)
    a_log = alog_ref[...][:, 0:1]
    dt_bias = dtb_ref[...][:, 0:1]
    xs = dec_in + dt_bias
    softplus = jnp.maximum(xs, 0.0) + jnp.log(1.0 + jnp.exp(-jnp.abs(xs)))
    g = -jnp.exp(a_log) * softplus
    gc_s[...] = _chunk_cumsum(jnp.broadcast_to(g, (tt, LANES)), LIN_CHUNK)
    bt_s[...] = jnp.broadcast_to(_sigmoid(beta_in), (tt, LANES))

    ri = lax.broadcasted_iota(jnp.int32, (LIN_CHUNK, LIN_CHUNK), 0)
    ci = lax.broadcasted_iota(jnp.int32, (LIN_CHUNK, LIN_CHUNK), 1)
    causal = ri >= ci
    strict = ri > ci

    for g0 in range(0, nc, GDN_CHUNK_GROUP):
        rows = [slice(c * LIN_CHUNK, (c + 1) * LIN_CHUNK) for c in range(g0, g0 + GDN_CHUNK_GROUP)]
        kc = [kn_s[r, :] for r in rows]
        gc = [gc_s[r, :] for r in rows]
        beta = [bt_s[r, :] for r in rows]
        decay = []
        for a in gc:
            diff = a[:, 0:LIN_CHUNK] - a.T[0:LIN_CHUNK, :]
            decay.append(jnp.where(causal, jnp.exp(jnp.where(causal, diff, 0.0)), 0.0))
        kb = [a * b for a, b in zip(kc, beta)]
        kcb = [a.astype(BF16) for a in kc]
        kk = [_dot_nt(a.astype(BF16), b) for a, b in zip(kb, kcb)]
        qk = [_dot_nt(qn_s[r, :].astype(BF16), b) for r, b in zip(rows, kcb)]
        lows = [jnp.where(strict, a * d, 0.0) for a, d in zip(kk, decay)]
        for r, a, d in zip(rows, qk, decay):
            at_s[r, :] = (a * d).astype(BF16)
        xinv = [a.astype(BF16) for a in _unit_lower_inverse_minus_eye(lows)]
        egc = [jnp.exp(a) for a in gc]
        vb = [vn_s[r, :] * b for r, b in zip(rows, beta)]
        kbe = [a * e for a, e in zip(kb, egc)]
        xu = [_dot(x, a.astype(BF16)) for x, a in zip(xinv, vb)]
        xw = [_dot(x, a.astype(BF16)) for x, a in zip(xinv, kbe)]
        for i, r in enumerate(rows):
            c = g0 + i
            u_s[r, :] = vb[i] + xu[i]
            w_s[r, :] = (kbe[i] + xw[i]).astype(BF16)
            qd_s[r, :] = (qn_s[r, :] * egc[i]).astype(BF16)
            g_last = gc[i][LIN_CHUNK - 1:LIN_CHUNK, :]
            kd_s[r, :] = (kc[i] * jnp.exp(g_last - gc[i])).astype(BF16)
            egl_s[c * SUBLANES:(c + 1) * SUBLANES, :] = jnp.broadcast_to(jnp.exp(g_last), (SUBLANES, LANES))


def _gdn_scan_kernel(u_ref, w_ref, at_ref, qd_ref, kd_ref, egl_ref, z_ref, ng_ref, o_ref, st_ref, *, tt):
    nc = tt // LIN_CHUNK
    hd = GDN_HEAD_DIM

    @pl.when(pl.program_id(0) == 0)
    def _():
        st_ref[...] = jnp.zeros_like(st_ref)

    ng = ng_ref[...]

    def scan_body(c, carry):
        r0 = pl.multiple_of(c * LIN_CHUNK, LIN_CHUNK)
        rows = pl.ds(r0, LIN_CHUNK)
        e0 = pl.multiple_of(c * SUBLANES, SUBLANES)
        for h in range(GDN_HEADS):
            cols = slice(h * hd, (h + 1) * hd)
            st = st_ref[h]
            stb = st.astype(BF16)
            v_new = u_ref[rows, cols] - _dot(w_ref[rows, cols], stb)
            vnb = v_new.astype(BF16)
            o = _dot(qd_ref[rows, cols], stb) + _dot(at_ref[h, rows, :], vnb)
            st_ref[h] = st * egl_ref[h, pl.ds(e0, 1), :] + _dot_tn(kd_ref[rows, cols], vnb)
            o = o * lax.rsqrt(jnp.mean(o * o, -1, keepdims=True) + RMS_EPS) * ng
            o_ref[rows, cols] = (o * _silu(z_ref[rows, cols])).astype(o_ref.dtype)
        return carry

    lax.fori_loop(0, nc, scan_body, 0)


def _gdn(proj, bd, conv_w, a_log, dt_bias, norm_g, tt_local, tt_scan):
    s = proj.shape[0]
    nh = GDN_HEADS
    hd = GDN_HEAD_DIM
    tt = tt_local
    per_head = lambda a: jnp.broadcast_to(a.astype(F32)[:, None, None], (nh, 1, LANES))
    blk = lambda off: pl.BlockSpec((tt, hd), lambda h, i: (i, off + h))
    cw = lambda off: pl.BlockSpec((GDN_CONV, hd), lambda h, i: (0, off + h))
    scal = pl.BlockSpec((None, 1, LANES), lambda h, i: (h, 0, 0))
    head_blk = pl.BlockSpec((tt, hd), lambda h, i: (i, h))
    u, w, at, qd, kd, egl = pl.pallas_call(
        functools.partial(_gdn_local_kernel, tt=tt),
        grid=(nh, s // tt),
        in_specs=[blk(0), blk(nh), blk(2 * nh),
                  pl.BlockSpec((tt, LANES), lambda h, i: (i, 0)),
                  cw(0), cw(nh), cw(2 * nh), scal, scal],
        out_specs=[head_blk, head_blk,
                   pl.BlockSpec((None, tt, LIN_CHUNK), lambda h, i: (h, i, 0)),
                   head_blk, head_blk,
                   pl.BlockSpec((None, tt // LIN_CHUNK * SUBLANES, LANES), lambda h, i: (h, i, 0))],
        out_shape=[jax.ShapeDtypeStruct((s, nh * hd), F32), jax.ShapeDtypeStruct((s, nh * hd), BF16),
                   jax.ShapeDtypeStruct((nh, s, LIN_CHUNK), BF16),
                   jax.ShapeDtypeStruct((s, nh * hd), BF16), jax.ShapeDtypeStruct((s, nh * hd), BF16),
                   jax.ShapeDtypeStruct((nh, s // LIN_CHUNK * SUBLANES, LANES), F32)],
        scratch_shapes=[pltpu.VMEM((SUBLANES, hd), F32), pltpu.VMEM((SUBLANES, hd), F32),
                        pltpu.VMEM((SUBLANES, hd), F32),
                        pltpu.VMEM((tt, hd), F32), pltpu.VMEM((tt, hd), F32), pltpu.VMEM((tt, hd), F32),
                        pltpu.VMEM((tt, LANES), F32), pltpu.VMEM((tt, LANES), F32)],
        compiler_params=_params(("parallel", "arbitrary")),
        name="gdn_local",
    )(proj, proj, proj, bd, conv_w, conv_w, conv_w, per_head(a_log), per_head(dt_bias))
    tt = tt_scan
    wide = pl.BlockSpec((tt, nh * hd), lambda i: (i, 0))
    return pl.pallas_call(
        functools.partial(_gdn_scan_kernel, tt=tt),
        grid=(s // tt,),
        in_specs=[wide, wide,
                  pl.BlockSpec((nh, tt, LIN_CHUNK), lambda i: (0, i, 0)),
                  wide, wide,
                  pl.BlockSpec((nh, tt // LIN_CHUNK * SUBLANES, LANES), lambda i: (0, i, 0)),
                  pl.BlockSpec((tt, nh * hd), lambda i: (i, 3)),
                  pl.BlockSpec((1, hd), lambda i: (0, 0))],
        out_specs=wide,
        out_shape=jax.ShapeDtypeStruct((s, nh * hd), BF16),
        scratch_shapes=[pltpu.VMEM((nh, hd, hd), F32)],
        compiler_params=_params(("arbitrary",)),
        name="gdn_scan",
    )(u, w, at, qd, kd, egl, proj, norm_g.reshape(1, hd))


S5_LANE_CHUNK = 512
S5_GROUPS_PER_TILE = LANES // S5_GROUP_CH


def _s5_kernel(u_ref, bre_ref, bim_ref, cre_ref, cim_ref, are_ref, aim_ref, pre_ref, pim_ref, d_ref,
               o_ref, nat_s, up_s, xr_s, xi_s, cr_s, ci_s, gr_s, gi_s, *, tt):
    nseg = SUBLANES
    seg = tt // nseg
    nstate = S5_GROUPS * S5_STATE
    ntile = S5_CHANNELS // LANES
    spt = S5_GROUPS_PER_TILE * S5_STATE

    @pl.when(pl.program_id(0) == 0)
    def _():
        gr_s[...] = jnp.zeros_like(gr_s)
        gi_s[...] = jnp.zeros_like(gi_s)

    for t in range(ntile):
        nat_s[t] = u_ref[:, t * LANES:(t + 1) * LANES]
    for t in range(ntile):
        for j in range(seg):
            up_s[t, j * nseg:(j + 1) * nseg, :] = nat_s[t, pl.ds(j, nseg, stride=seg), :]
    for t in range(ntile):
        ub = up_s[t].astype(BF16)
        xr_s[:, t * spt:(t + 1) * spt] = _dot(ub, bre_ref[t])
        xi_s[:, t * spt:(t + 1) * spt] = _dot(ub, bim_ref[t])

    for lc in range(nstate // S5_LANE_CHUNK):
        cols = slice(lc * S5_LANE_CHUNK, (lc + 1) * S5_LANE_CHUNK)
        ar = are_ref[:, cols]
        ai = aim_ref[:, cols]

        def local_step(j, carry):
            xr, xi = carry
            r0 = pl.multiple_of(j * nseg, nseg)
            nr = ar * xr - ai * xi + xr_s[pl.ds(r0, nseg), cols]
            ni = ar * xi + ai * xr + xi_s[pl.ds(r0, nseg), cols]
            xr_s[pl.ds(r0, nseg), cols] = nr
            xi_s[pl.ds(r0, nseg), cols] = ni
            return nr, ni

        zero = jnp.zeros((nseg, S5_LANE_CHUNK), F32)
        er, ei = lax.fori_loop(0, seg, local_step, (zero, zero))
        pr = pre_ref[seg - 1:seg, cols]
        pi = pim_ref[seg - 1:seg, cols]
        cr = gr_s[:, cols]
        ci = gi_s[:, cols]
        for r in range(nseg):
            cr_s[r:r + 1, cols] = cr
            ci_s[r:r + 1, cols] = ci
            cr, ci = (pr * cr - pi * ci + er[r:r + 1], pr * ci + pi * cr + ei[r:r + 1])
        gr_s[:, cols] = cr
        gi_s[:, cols] = ci
        c_re = cr_s[:, cols]
        c_im = ci_s[:, cols]

        def fix_step(j, carry):
            r0 = pl.multiple_of(j * nseg, nseg)
            pjr = pre_ref[pl.ds(j, 1), cols]
            pji = pim_ref[pl.ds(j, 1), cols]
            xr_s[pl.ds(r0, nseg), cols] = xr_s[pl.ds(r0, nseg), cols] + pjr * c_re - pji * c_im
            xi_s[pl.ds(r0, nseg), cols] = xi_s[pl.ds(r0, nseg), cols] + pjr * c_im + pji * c_re
            return carry

        lax.fori_loop(0, seg, fix_step, 0)

    for t in range(ntile):
        xr = xr_s[:, t * spt:(t + 1) * spt].astype(BF16)
        xi = xi_s[:, t * spt:(t + 1) * spt].astype(BF16)
        y = _dot(xr, cre_ref[t]) - _dot(xi, cim_ref[t]) + d_ref[:, t * LANES:(t + 1) * LANES] * up_s[t]
        up_s[t] = 0.5 * y * (1.0 + jnp.tanh(math.sqrt(2.0 / math.pi) * (y + 0.044715 * (y * y * y))))
    for t in range(ntile):
        for j in range(seg):
            nat_s[t, pl.ds(j, nseg, stride=seg), :] = up_s[t, j * nseg:(j + 1) * nseg, :]
    for t in range(ntile):
        o_ref[:, t * LANES:(t + 1) * LANES] = nat_s[t]


def _s5(proj, col0, a_re, a_im, b_re, b_im, c_re, c_im, d, log_step, tt):
    s = proj.shape[0]
    seg = tt // SUBLANES
    nstate = S5_GROUPS * S5_STATE
    ntile = S5_CHANNELS // LANES
    gpt = S5_GROUPS_PER_TILE
    a_re, a_im, b_re, b_im, c_re, c_im, d = (t.astype(F32) for t in (a_re, a_im, b_re, b_im, c_re, c_im, d))
    step = jnp.exp(log_step.astype(F32))[:, None]
    mag = jnp.exp(a_re * step)
    ab_re = mag * jnp.cos(a_im * step)
    ab_im = mag * jnp.sin(a_im * step)
    den = jnp.square(a_re) + jnp.square(a_im)
    z_re = ((ab_re - 1.0) * a_re + ab_im * a_im) / den
    z_im = (ab_im * a_re - (ab_re - 1.0) * a_im) / den
    bb_re = z_re[..., None] * b_re - z_im[..., None] * b_im
    bb_im = z_re[..., None] * b_im + z_im[..., None] * b_re

    def block_diag_in(bb):
        bt = bb.reshape(ntile, gpt, S5_STATE, S5_GROUP_CH)
        eye = jnp.eye(gpt, dtype=F32)
        m = jnp.einsum('tgpc,gh->tgchp', bt, eye)
        return m.reshape(ntile, gpt * S5_GROUP_CH, gpt * S5_STATE).astype(BF16)

    def block_diag_out(cc):
        ct = cc.reshape(ntile, gpt, S5_GROUP_CH, S5_STATE)
        eye = jnp.eye(gpt, dtype=F32)
        m = jnp.einsum('tgcp,gh->tgphc', ct, eye)
        return m.reshape(ntile, gpt * S5_STATE, gpt * S5_GROUP_CH).astype(BF16)

    def pow_step(carry, _):
        pr, pi = carry
        nr, ni = pr * ab_re - pi * ab_im, pr * ab_im + pi * ab_re
        return (nr, ni), (pr, pi)

    _, (pw_re, pw_im) = lax.scan(pow_step, (ab_re, ab_im), None, length=seg)
    pw_re = pw_re.reshape(seg, nstate)
    pw_im = pw_im.reshape(seg, nstate)
    full = lambda shape: pl.BlockSpec(shape, lambda i: (0,) * len(shape))
    return pl.pallas_call(
        functools.partial(_s5_kernel, tt=tt),
        grid=(s // tt,),
        in_specs=[pl.BlockSpec((tt, S5_CHANNELS), lambda i: (i, col0)),
                  full((ntile, LANES, gpt * S5_STATE)), full((ntile, LANES, gpt * S5_STATE)),
                  full((ntile, gpt * S5_STATE, LANES)), full((ntile, gpt * S5_STATE, LANES)),
                  full((1, nstate)), full((1, nstate)),
                  full((seg, nstate)), full((seg, nstate)),
                  full((1, S5_CHANNELS))],
        out_specs=pl.BlockSpec((tt, S5_CHANNELS), lambda i: (i, 0)),
        out_shape=jax.ShapeDtypeStruct((s, S5_CHANNELS), F32),
        scratch_shapes=[pltpu.VMEM((ntile, tt, LANES), F32), pltpu.VMEM((ntile, tt, LANES), F32),
                        pltpu.VMEM((tt, nstate), F32), pltpu.VMEM((tt, nstate), F32),
                        pltpu.VMEM((SUBLANES, nstate), F32), pltpu.VMEM((SUBLANES, nstate), F32),
                        pltpu.VMEM((1, nstate), F32), pltpu.VMEM((1, nstate), F32)],
        compiler_params=_params(("arbitrary",)),
        name="s5",
    )(proj, block_diag_in(bb_re), block_diag_in(bb_im), block_diag_out(c_re), block_diag_out(c_im),
      ab_re.reshape(1, nstate), ab_im.reshape(1, nstate), pw_re, pw_im, d.reshape(1, S5_CHANNELS))


def _glu_kernel(y_ref, w_ref, b_ref, o_ref):
    y = y_ref[...]
    t = _dot(y.astype(BF16), w_ref[...]) + b_ref[...]
    o_ref[...] = (y * _sigmoid(t)).astype(o_ref.dtype)


def _glu(y, w, b, tm):
    m, n = y.shape
    return pl.pallas_call(
        _glu_kernel,
        grid=(m // tm,),
        in_specs=[pl.BlockSpec((tm, n), lambda i: (i, 0)),
                  pl.BlockSpec((n, n), lambda i: (0, 0)),
                  pl.BlockSpec((1, n), lambda i: (0, 0))],
        out_specs=pl.BlockSpec((tm, n), lambda i: (i, 0)),
        out_shape=jax.ShapeDtypeStruct((m, n), BF16),
        compiler_params=_params(("parallel",)),
        name="glu",
    )(y, w, b.reshape(1, n))


def _pad_cols(w, n):
    return jnp.concatenate([w, jnp.zeros((w.shape[0], n - w.shape[1]), w.dtype)], axis=1)


def _split_cols(w, sizes):
    offs = [0]
    for sz in sizes:
        offs.append(offs[-1] + sz)
    return [w[:, offs[i]:offs[i + 1]] for i in range(len(sizes))]


def _even_mixer(xb, w_in, w_gate2, b_gate, norm_g):
    hk, hv, hm = GLA_HEADS * GLA_DK, GLA_HEADS * GLA_DV, MOBA_HEADS * MOBA_HEAD_DIM
    gq, gk, gv, g_lr, gr, mq, mk, mv = _split_cols(w_in, (hk, hk, hv, GLA_GATE_RANK, hv, hm, hm, hm))
    w_main = jnp.concatenate([gq, gk, gv, gr, mq, mk, mv], axis=1).astype(BF16)
    proj = _matmul(xb, w_main, F32, 1024, 1024)
    glr = _matmul(xb, _pad_cols(g_lr, LANES).astype(BF16), F32, 1024, LANES)
    gla_out = _gla(proj, glr, w_gate2, b_gate, norm_g, 512)
    moba_out = _moba(proj, (2 * hk + 2 * hv) // LANES)
    return jnp.concatenate([gla_out, moba_out], axis=-1)


def _odd_mixer(xb, w_in, conv_w, a_log, dt_bias, norm_g, a_re, a_im, b_re, b_im, c_re, c_im, d, log_step,
               glu_w, glu_b):
    hq = GDN_HEADS * GDN_HEAD_DIM
    qkv, z, beta, decay, u = _split_cols(w_in, (3 * hq, hq, GDN_HEADS, GDN_HEADS, S5_CHANNELS))
    w_main = jnp.concatenate([qkv, z, u], axis=1).astype(BF16)
    proj = _matmul(xb, w_main, F32, 1024, 1024)
    bd = _matmul(xb, _pad_cols(jnp.concatenate([beta, decay], axis=1), LANES).astype(BF16), F32, 1024, LANES)
    gdn_out = _gdn(proj, bd, conv_w, a_log, dt_bias, norm_g, 512, 512)
    y = _s5(proj, 4 * hq // S5_CHANNELS, a_re, a_im, b_re, b_im, c_re, c_im, d, log_step, 512)
    s5_out = _glu(y, glu_w.astype(BF16), glu_b, 512)
    return jnp.concatenate([gdn_out, s5_out], axis=-1)


def kernel(x, even_w_in, gla_w_gate2, gla_b_gate, gla_norm_g, even_w_out, odd_w_in, gdn_conv_w, gdn_a_log,
           gdn_dt_bias, gdn_norm_g, s5_a_re, s5_a_im, s5_b_re, s5_b_im, s5_c_re, s5_c_im, s5_d, s5_log_step,
           s5_glu_w, s5_glu_b, odd_w_out, ln_mix_g, ln_mix_b, ffn_w_up, ffn_conv_w, ffn_w_down, ln_ffn_g, ln_ffn_b):
    bsz, seq, dm = x.shape
    outs = []
    for b in range(bsz):
        xf = x[b]
        xb = xf.astype(BF16)
        for i in range(DEPTH):
            j = i // 2
            if i % 2 == 0:
                mix_in = _even_mixer(xb, even_w_in[j], gla_w_gate2[j], gla_b_gate[j], gla_norm_g[j])
                w_out = even_w_out[j]
            else:
                mix_in = _odd_mixer(xb, odd_w_in[j], gdn_conv_w[j], gdn_a_log[j], gdn_dt_bias[j], gdn_norm_g[j],
                                    s5_a_re[j], s5_a_im[j], s5_b_re[j], s5_b_im[j], s5_c_re[j], s5_c_im[j],
                                    s5_d[j], s5_log_step[j], s5_glu_w[j], s5_glu_b[j])
                w_out = odd_w_out[j]
            xf, xb = _matmul_res_ln(mix_in, w_out.astype(BF16), xf, ln_mix_g[i], ln_mix_b[i], 512, 512)
            mid = _ffn_up(xb, ffn_w_up[i].astype(BF16), ffn_conv_w[i], 1024, 512)
            xf, xb = _matmul_res_ln(mid, ffn_w_down[i].astype(BF16), xf, ln_ffn_g[i], ln_ffn_b[i], 512, 512)
        outs.append(xf)
    return jnp.stack(outs, axis=0)
```

```python
import functools
import math

import jax
import jax.numpy as jnp
from jax import lax
from jax.experimental import pallas as pl
from jax.experimental.pallas import tpu as pltpu

F32 = jnp.float32
BF16 = jnp.bfloat16
HIGHEST = lax.Precision.HIGHEST

DEPTH = 4
GLA_HEADS, GLA_DK, GLA_DV, GLA_GATE_RANK, GLA_GATE_NORM = 8, 64, 128, 16, 16.0
MOBA_HEADS, MOBA_HEAD_DIM, MOBA_BLOCK, MOBA_TOPK = 8, 128, 256, 3
GDN_HEADS, GDN_HEAD_DIM, GDN_CONV = 8, 128, 4
S5_CHANNELS, S5_GROUP_CH, S5_STATE = 1024, 16, 64
S5_GROUPS = S5_CHANNELS // S5_GROUP_CH
LIN_CHUNK = 64
D_FF = 5632
FFN_CONV = 3
DEEPNORM_ALPHA = (2 * DEPTH) ** 0.25
LN_EPS = 1e-5
RMS_EPS = 1e-6

LANES = 128
SUBLANES = 8
VMEM_LIMIT_BYTES = 56 * 1024 * 1024
NEG_BIG = -1e30


def _params(semantics):
    return pltpu.CompilerParams(dimension_semantics=semantics, vmem_limit_bytes=VMEM_LIMIT_BYTES)


def _dot(a, b, precision=None):
    return jnp.dot(a, b, preferred_element_type=F32, precision=precision)


def _dot_nt(a, b, precision=None):
    return lax.dot_general(a, b, (((1,), (1,)), ((), ())), preferred_element_type=F32, precision=precision)


def _dot_tn(a, b, precision=None):
    return lax.dot_general(a, b, (((0,), (0,)), ((), ())), preferred_element_type=F32, precision=precision)


def _sigmoid(x):
    return 1.0 / (1.0 + jnp.exp(-x))


def _silu(x):
    return x * _sigmoid(x)


def _chunk_cumsum(x, chunk):
    pos = lax.broadcasted_iota(jnp.int32, x.shape, 0) % chunk
    k = 1
    while k < chunk:
        x = x + jnp.where(pos >= k, pltpu.roll(x, k, 0), 0.0)
        k *= 2
    return x


def _mm_kernel(a_ref, b_ref, o_ref):
    o_ref[...] = _dot(a_ref[...].astype(BF16), b_ref[...]).astype(o_ref.dtype)


def _matmul(a, b, out_dtype, tm, tn):
    m, k = a.shape
    n = b.shape[1]
    return pl.pallas_call(
        _mm_kernel,
        grid=(m // tm, n // tn),
        in_specs=[pl.BlockSpec((tm, k), lambda i, j: (i, 0)),
                  pl.BlockSpec((k, tn), lambda i, j: (0, j))],
        out_specs=pl.BlockSpec((tm, tn), lambda i, j: (i, j)),
        out_shape=jax.ShapeDtypeStruct((m, n), out_dtype),
        compiler_params=_params(("parallel", "parallel")),
        name="matmul",
    )(a, b)


def _mm_ln_kernel(a_ref, b_ref, x_ref, g_ref, bt_ref, of_ref, ob_ref, acc_ref, *, nk):
    k = pl.program_id(1)

    @pl.when(k == 0)
    def _():
        acc_ref[...] = jnp.zeros_like(acc_ref)

    acc_ref[...] += _dot(a_ref[...], b_ref[...])

    @pl.when(k == nk - 1)
    def _():
        y = DEEPNORM_ALPHA * x_ref[...] + acc_ref[...]
        mu = jnp.mean(y, -1, keepdims=True)
        d = y - mu
        var = jnp.mean(d * d, -1, keepdims=True)
        o = d * lax.rsqrt(var + LN_EPS) * g_ref[...] + bt_ref[...]
        of_ref[...] = o
        ob_ref[...] = o.astype(BF16)


def _matmul_res_ln(a, b, x, g, bt, tm, tk):
    m, k = a.shape
    n = b.shape[1]
    nk = k // tk
    return pl.pallas_call(
        functools.partial(_mm_ln_kernel, nk=nk),
        grid=(m // tm, nk),
        in_specs=[pl.BlockSpec((tm, tk), lambda i, kk: (i, kk)),
                  pl.BlockSpec((tk, n), lambda i, kk: (kk, 0)),
                  pl.BlockSpec((tm, n), lambda i, kk: (i, 0)),
                  pl.BlockSpec((1, n), lambda i, kk: (0, 0)),
                  pl.BlockSpec((1, n), lambda i, kk: (0, 0))],
        out_specs=[pl.BlockSpec((tm, n), lambda i, kk: (i, 0)),
                   pl.BlockSpec((tm, n), lambda i, kk: (i, 0))],
        out_shape=[jax.ShapeDtypeStruct((m, n), F32), jax.ShapeDtypeStruct((m, n), BF16)],
        scratch_shapes=[pltpu.VMEM((tm, n), F32)],
        compiler_params=_params(("parallel", "arbitrary")),
        name="matmul_res_ln",
    )(a, b, x, g.reshape(1, n), bt.reshape(1, n))


def _causal_conv_rows(h, w, prev):
    width = w.shape[0]
    out = w[width - 1:width] * h
    top = jnp.concatenate([prev, h[0:SUBLANES]], axis=0)
    out_top = w[width - 1:width] * top
    for s in range(1, width):
        wj = w[width - 1 - s:width - s]
        out = out + wj * pltpu.roll(h, s, 0)
        out_top = out_top + wj * pltpu.roll(top, s, 0)
    return out, out_top[SUBLANES:]


def _ffn_up_kernel(x_ref, wg_ref, wv_ref, cg_ref, cv_ref, o_ref, carry_g, carry_v):
    @pl.when(pl.program_id(1) == 0)
    def _():
        carry_g[...] = jnp.zeros_like(carry_g)
        carry_v[...] = jnp.zeros_like(carry_v)

    x = x_ref[...]
    tm = x.shape[0]
    hg = _dot(x, wg_ref[...])
    hv = _dot(x, wv_ref[...])
    g, g_top = _causal_conv_rows(hg, cg_ref[...], carry_g[...])
    v, v_top = _causal_conv_rows(hv, cv_ref[...], carry_v[...])
    carry_g[...] = hg[tm - SUBLANES:]
    carry_v[...] = hv[tm - SUBLANES:]
    o_ref[...] = (_silu(g) * v).astype(o_ref.dtype)
    o_ref[0:SUBLANES, :] = (_silu(g_top) * v_top).astype(o_ref.dtype)


def _ffn_up(xb, w_up, conv_w, tm, tn):
    m, k = xb.shape
    d_ff = w_up.shape[1] // 2
    nt = d_ff // tn
    return pl.pallas_call(
        _ffn_up_kernel,
        grid=(nt, m // tm),
        in_specs=[pl.BlockSpec((tm, k), lambda j, i: (i, 0)),
                  pl.BlockSpec((k, tn), lambda j, i: (0, j)),
                  pl.BlockSpec((k, tn), lambda j, i: (0, j + nt)),
                  pl.BlockSpec((FFN_CONV, tn), lambda j, i: (0, j)),
                  pl.BlockSpec((FFN_CONV, tn), lambda j, i: (0, j + nt))],
        out_specs=pl.BlockSpec((tm, tn), lambda j, i: (i, j)),
        out_shape=jax.ShapeDtypeStruct((m, d_ff), BF16),
        scratch_shapes=[pltpu.VMEM((SUBLANES, tn), F32), pltpu.VMEM((SUBLANES, tn), F32)],
        compiler_params=_params(("parallel", "arbitrary")),
        name="ffn_up",
    )(xb, w_up, w_up, conv_w, conv_w)


def _gla_kernel(q_ref, k_ref, v_ref, r_ref, glr_ref, wg2_ref, bg_ref, ng_ref, o_ref,
                st_ref, qe_s, ke_s, kd_s, vb_s, ecl_s, *, tt):
    nc = tt // LIN_CHUNK
    hk = GLA_HEADS * GLA_DK

    @pl.when(pl.program_id(0) == 0)
    def _():
        st_ref[...] = jnp.zeros_like(st_ref)

    z = _dot(glr_ref[...], wg2_ref[...], HIGHEST) + bg_ref[...]
    log_a = (jnp.minimum(z, 0.0) - jnp.log(1.0 + jnp.exp(-jnp.abs(z)))) / GLA_GATE_NORM
    cum = _chunk_cumsum(log_a, LIN_CHUNK)
    q = q_ref[...] * (GLA_DK ** -0.5)
    k = k_ref[...]
    qe_s[...] = (q * jnp.exp(cum)).astype(BF16)
    ke_s[...] = (k * jnp.exp(-cum)).astype(BF16)
    for c in range(nc):
        rows = slice(c * LIN_CHUNK, (c + 1) * LIN_CHUNK)
        cl = cum[(c + 1) * LIN_CHUNK - 1:(c + 1) * LIN_CHUNK, :]
        kd_s[rows, :] = (k[rows] * jnp.exp(cl - cum[rows])).astype(BF16)
        ecl_s[c * SUBLANES:(c + 1) * SUBLANES, :] = jnp.broadcast_to(jnp.exp(cl), (SUBLANES, hk))
    vb_s[...] = v_ref[...].astype(BF16)

    ri = lax.broadcasted_iota(jnp.int32, (LIN_CHUNK, LIN_CHUNK), 0)
    ci = lax.broadcasted_iota(jnp.int32, (LIN_CHUNK, LIN_CHUNK), 1)
    causal = ri >= ci
    lane = lax.broadcasted_iota(jnp.int32, (LIN_CHUNK, LANES), 1)
    lo_half = lane < GLA_DK
    ng = ng_ref[...]

    def chunk_body(c, carry):
        r0 = pl.multiple_of(c * LIN_CHUNK, LIN_CHUNK)
        e0 = pl.multiple_of(c * SUBLANES, SUBLANES)
        for h in range(GLA_HEADS):
            pcol = slice((h // 2) * LANES, (h // 2 + 1) * LANES)
            keep = lo_half if h % 2 == 0 else jnp.logical_not(lo_half)
            qm = jnp.where(keep, qe_s[pl.ds(r0, LIN_CHUNK), pcol], 0)
            kdm = jnp.where(keep, kd_s[pl.ds(r0, LIN_CHUNK), pcol], 0)
            ke = ke_s[pl.ds(r0, LIN_CHUNK), pcol]
            vh = vb_s[pl.ds(r0, LIN_CHUNK), h * GLA_DV:(h + 1) * GLA_DV]
            st = st_ref[h]
            attn = jnp.where(causal, _dot_nt(qm, ke), 0.0)
            o = _dot(attn.astype(BF16), vh) + _dot_nt(qm, st.astype(BF16))
            st_ref[h] = st * ecl_s[pl.ds(e0, 1), pcol] + _dot_tn(vh, kdm)
            o = o * lax.rsqrt(jnp.mean(o * o, -1, keepdims=True) + RMS_EPS) * ng
            gate = r_ref[pl.ds(r0, LIN_CHUNK), h * GLA_DV:(h + 1) * GLA_DV]
            o_ref[pl.ds(r0, LIN_CHUNK), h * GLA_DV:(h + 1) * GLA_DV] = (o * _silu(gate)).astype(o_ref.dtype)
        return carry

    lax.fori_loop(0, nc, chunk_body, 0)


def _gla(proj, glr, w_gate2, b_gate, norm_g, tt):
    s = proj.shape[0]
    hk = GLA_HEADS * GLA_DK
    hv = GLA_HEADS * GLA_DV
    wg2 = jnp.zeros((LANES, hk), F32).at[:GLA_GATE_RANK].set(w_gate2)
    return pl.pallas_call(
        functools.partial(_gla_kernel, tt=tt),
        grid=(s // tt,),
        in_specs=[pl.BlockSpec((tt, hk), lambda i: (i, 0)),
                  pl.BlockSpec((tt, hk), lambda i: (i, 1)),
                  pl.BlockSpec((tt, hv), lambda i: (i, 1)),
                  pl.BlockSpec((tt, hv), lambda i: (i, 2)),
                  pl.BlockSpec((tt, LANES), lambda i: (i, 0)),
                  pl.BlockSpec((LANES, hk), lambda i: (0, 0)),
                  pl.BlockSpec((1, hk), lambda i: (0, 0)),
                  pl.BlockSpec((1, GLA_DV), lambda i: (0, 0))],
        out_specs=pl.BlockSpec((tt, hv), lambda i: (i, 0)),
        out_shape=jax.ShapeDtypeStruct((s, hv), BF16),
        scratch_shapes=[pltpu.VMEM((GLA_HEADS, GLA_DV, LANES), F32),
                        pltpu.VMEM((tt, hk), BF16), pltpu.VMEM((tt, hk), BF16), pltpu.VMEM((tt, hk), BF16),
                        pltpu.VMEM((tt, hv), BF16),
                        pltpu.VMEM((tt // LIN_CHUNK * SUBLANES, hk), F32)],
        compiler_params=_params(("arbitrary",)),
        name="gla",
    )(proj, proj, proj, proj, glr, wg2, b_gate.reshape(1, hk), norm_g.reshape(1, GLA_DV))


def _split3(x):
    hi = x.astype(BF16).astype(F32)
    rest = x - hi
    mid = rest.astype(BF16).astype(F32)
    lo = (rest - mid).astype(BF16).astype(F32)
    return hi, mid, lo


MOBA_SEL_LANES = 32
MOBA_ROW_LANE = 96
MOBA_COL_LANE = 99
LOG2E = 1.4426950408889634
MOBA_GROUP = 4


def _moba_kernel(q_ref, k_ref, v_ref, slope_ref, o_ref, kmean_s, kx_s, vt_s, qxt_s, s_s, acc_s, *, n_blk):
    blk = MOBA_BLOCK
    hd = MOBA_HEAD_DIM
    i = pl.program_id(1)
    sl2 = slope_ref[...][:, 0:1] * LOG2E
    s_hi, s_mid, s_lo = _split3(sl2)

    @pl.when(i == 0)
    def _():
        lane = lax.broadcasted_iota(jnp.int32, (blk, LANES), 1)
        key_f = lax.broadcasted_iota(jnp.int32, (blk, LANES), 0).astype(F32)
        k_extra = jnp.where(lane == MOBA_ROW_LANE, -s_hi, jnp.where(lane == MOBA_ROW_LANE + 1, -s_mid,
                  jnp.where(lane == MOBA_ROW_LANE + 2, -s_lo,
                  jnp.where(jnp.logical_and(lane >= MOBA_COL_LANE, lane < MOBA_COL_LANE + 3), key_f, 0.0))))
        kmean_s[...] = jnp.zeros_like(kmean_s)
        for n in range(n_blk):
            rows = slice(n * blk, (n + 1) * blk)
            kb = k_ref[rows, :]
            kmean_s[n:n + 1, :] = jnp.mean(kb, axis=0, keepdims=True)
            onehot = jnp.logical_or(lane == n, jnp.logical_or(lane == MOBA_SEL_LANES + n,
                                                               lane == 2 * MOBA_SEL_LANES + n))
            kx_s[rows, 0:hd] = kb.astype(BF16)
            kx_s[rows, hd:] = jnp.where(onehot, 1.0, k_extra).astype(BF16)
            gcols = slice((n % MOBA_GROUP) * blk, (n % MOBA_GROUP + 1) * blk)
            vt_s[n // MOBA_GROUP, :, gcols] = v_ref[rows, :].T.astype(BF16)

    q = q_ref[...]
    nb = lax.broadcasted_iota(jnp.int32, (MOBA_SEL_LANES, blk), 0)
    nb_f = nb.astype(F32)
    gate = _dot_nt(kmean_s[...], q, HIGHEST)
    gate = jnp.where(nb < i, gate, -jnp.inf)
    chosen = jnp.zeros((MOBA_SEL_LANES, blk), jnp.bool_)
    for _ in range(MOBA_TOPK):
        best = jnp.max(gate, axis=0, keepdims=True)
        idx = jnp.min(jnp.where(gate == best, nb_f, float(MOBA_SEL_LANES)), axis=0, keepdims=True)
        hit = nb_f == idx
        chosen = jnp.logical_or(chosen, hit)
        gate = jnp.where(hit, -jnp.inf, gate)
    sel = jnp.where(jnp.logical_and(chosen, nb < i), -sl2 * ((i - nb) * blk).astype(F32), NEG_BIG)
    sel = jnp.where(nb == i, 0.0, sel)
    b_hi, b_mid, b_lo = _split3(sel)
    qry_f = lax.broadcasted_iota(jnp.int32, (MOBA_SEL_LANES, blk), 1).astype(F32)
    tail = jnp.where(nb < 3, qry_f, jnp.where(nb == 3, s_hi, jnp.where(nb == 4, s_mid,
           jnp.where(nb == 5, s_lo, 0.0))))
    qxt_s[0:hd, :] = (q * (hd ** -0.5 * LOG2E)).T.astype(BF16)
    qxt_s[hd:, :] = jnp.concatenate([b_hi, b_mid, b_lo, tail], axis=0).astype(BF16)

    ngrp = i // MOBA_GROUP + 1
    gkeys = MOBA_GROUP * blk

    def score_group(g, carry):
        k0 = pl.multiple_of(g * gkeys, gkeys)
        s = _dot(kx_s[pl.ds(k0, gkeys), :], qxt_s[...])
        s_s[pl.ds(g * MOBA_GROUP, MOBA_GROUP)] = s.reshape(MOBA_GROUP, blk, blk)
        return carry

    lax.fori_loop(0, ngrp, score_group, 0)
    key = lax.broadcasted_iota(jnp.int32, (blk, blk), 0)
    qry = lax.broadcasted_iota(jnp.int32, (blk, blk), 1)
    s_s[i] = jnp.where(key <= qry, s_s[i], NEG_BIG)

    def fold(x):
        return x.reshape(x.shape[0] // SUBLANES, SUBLANES, blk)

    def max_group(g, mx):
        s = s_s[pl.ds(g * MOBA_GROUP, MOBA_GROUP)].reshape(gkeys, blk)
        return jnp.maximum(mx, jnp.max(fold(s), axis=0))

    mx = lax.fori_loop(0, ngrp, max_group, jnp.full((SUBLANES, blk), NEG_BIG, F32))
    m_row = jnp.max(mx, axis=0, keepdims=True)

    acc_s[...] = jnp.zeros_like(acc_s)

    def prob_group(g, ls):
        s = s_s[pl.ds(g * MOBA_GROUP, MOBA_GROUP)].reshape(gkeys, blk)
        p = jnp.exp2(s - m_row)
        acc_s[...] += _dot(vt_s[g], p.astype(BF16))
        return ls + jnp.sum(fold(p), axis=0)

    ls = lax.fori_loop(0, ngrp, prob_group, jnp.zeros((SUBLANES, blk), F32))
    l_row = jnp.sum(ls, axis=0, keepdims=True)
    o_ref[...] = (acc_s[...] / l_row).T.astype(o_ref.dtype)


def _moba(proj, col0):
    s = proj.shape[0]
    n_blk = s // MOBA_BLOCK
    assert n_blk <= MOBA_SEL_LANES
    slopes = jnp.exp2(-8.0 * jnp.arange(1, MOBA_HEADS + 1, dtype=F32) / MOBA_HEADS)
    slopes = jnp.broadcast_to(slopes[:, None, None], (MOBA_HEADS, 1, LANES))
    return pl.pallas_call(
        functools.partial(_moba_kernel, n_blk=n_blk),
        grid=(MOBA_HEADS, n_blk),
        in_specs=[pl.BlockSpec((MOBA_BLOCK, MOBA_HEAD_DIM), lambda h, i: (i, col0 + h)),
                  pl.BlockSpec((s, MOBA_HEAD_DIM), lambda h, i: (0, col0 + MOBA_HEADS + h)),
                  pl.BlockSpec((s, MOBA_HEAD_DIM), lambda h, i: (0, col0 + 2 * MOBA_HEADS + h)),
                  pl.BlockSpec((None, 1, LANES), lambda h, i: (h, 0, 0))],
        out_specs=pl.BlockSpec((MOBA_BLOCK, MOBA_HEAD_DIM), lambda h, i: (i, h)),
        out_shape=jax.ShapeDtypeStruct((s, MOBA_HEADS * MOBA_HEAD_DIM), BF16),
        scratch_shapes=[pltpu.VMEM((MOBA_SEL_LANES, MOBA_HEAD_DIM), F32),
                        pltpu.VMEM((s, MOBA_HEAD_DIM + LANES), BF16),
                        pltpu.VMEM((n_blk // MOBA_GROUP, MOBA_HEAD_DIM, MOBA_GROUP * MOBA_BLOCK), BF16),
                        pltpu.VMEM((MOBA_HEAD_DIM + LANES, MOBA_BLOCK), BF16),
                        pltpu.VMEM((n_blk, MOBA_BLOCK, MOBA_BLOCK), F32),
                        pltpu.VMEM((MOBA_HEAD_DIM, MOBA_BLOCK), F32)],
        compiler_params=_params(("parallel", "arbitrary")),
        name="moba",
    )(proj, proj, proj, slopes)


GDN_CHUNK_GROUP = 8


def _unit_lower_inverse_minus_eye(lows):
    n = lows[0].shape[0]
    r = lax.broadcasted_iota(jnp.int32, (n, n), 0)
    c = lax.broadcasted_iota(jnp.int32, (n, n), 1)
    eye = jnp.where(r == c, 1.0, 0.0)
    same16 = (r // 16) == (c // 16)
    same32 = (r // 32) == (c // 32)
    bdot = lambda a, b: [_dot(x.astype(BF16), y.astype(BF16)) for x, y in zip(a, b)]
    ld = [jnp.where(same16, m, 0.0) for m in lows]
    p2 = bdot(ld, ld)
    p4 = bdot(p2, p2)
    p8 = bdot(p4, p4)
    lp = bdot(ld, p2)
    x = [a - b - c_ for a, b, c_ in zip(p2, ld, lp)]
    xp = bdot(x, p4)
    x = [a + b + c_ for a, b, c_ in zip(x, p4, xp)]
    xp = bdot(x, p8)
    t = [eye + a + b + c_ for a, b, c_ in zip(x, p8, xp)]
    c32 = [jnp.where(jnp.logical_and(same32, jnp.logical_not(same16)), m, 0.0) for m in lows]
    t = [a - b for a, b in zip(t, bdot(bdot(t, c32), t))]
    c64 = [jnp.where(same32, 0.0, m) for m in lows]
    t = [a - b for a, b in zip(t, bdot(bdot(t, c64), t))]
    return [a - eye for a in t]


def _gdn_local_kernel(q_ref, k_ref, v_ref, bd_ref, cwq_ref, cwk_ref, cwv_ref, alog_ref, dtb_ref,
                      u_s, w_s, at_s, qd_s, kd_s, egl_s, cq_s, ck_s, cv_s, qn_s, kn_s, vn_s, gc_s, bt_s, *, tt):
    nc = tt // LIN_CHUNK
    hd = GDN_HEAD_DIM
    h = pl.program_id(0)

    @pl.when(pl.program_id(1) == 0)
    def _():
        cq_s[...] = jnp.zeros_like(cq_s)
        ck_s[...] = jnp.zeros_like(ck_s)
        cv_s[...] = jnp.zeros_like(cv_s)

    def conv_silu(x_ref, cw_ref, carry, dst):
        x = x_ref[...]
        y, y_top = _causal_conv_rows(x, cw_ref[...], carry[...])
        carry[...] = x[tt - SUBLANES:]
        dst[...] = _silu(y)
        dst[0:SUBLANES, :] = _silu(y_top)

    conv_silu(q_ref, cwq_ref, cq_s, qn_s)
    conv_silu(k_ref, cwk_ref, ck_s, kn_s)
    conv_silu(v_ref, cwv_ref, cv_s, vn_s)
    q = qn_s[...]
    k = kn_s[...]
    qn_s[...] = q * lax.rsqrt(jnp.sum(q * q, -1, keepdims=True) + RMS_EPS) * (hd ** -0.5)
    kn_s[...] = k * lax.rsqrt(jnp.sum(k * k, -1, keepdims=True) + RMS_EPS)

    bd = bd_ref[...]
    lane = lax.broadcasted_iota(jnp.int32, bd.shape, 1)
    beta_in = jnp.sum(jnp.where(lane == h, bd, 0.0), -1, keepdims=True)
    dec_in = jnp.sum(jnp.where(lane == GDN_HEADS + h, bd, 0.0), -1, keepdims=True)
    a_log = alog_ref[...][:, 0:1]
    dt_bias = dtb_ref[...][:, 0:1]
    xs = dec_in + dt_bias
    softplus = jnp.maximum(xs, 0.0) + jnp.log(1.0 + jnp.exp(-jnp.abs(xs)))
    g = -jnp.exp(a_log) * softplus
    gc_s[...] = _chunk_cumsum(jnp.broadcast_to(g, (tt, LANES)), LIN_CHUNK)
    bt_s[...] = jnp.broadcast_to(_sigmoid(beta_in), (tt, LANES))

    ri = lax.broadcasted_iota(jnp.int32, (LIN_CHUNK, LIN_CHUNK), 0)
    ci = lax.broadcasted_iota(jnp.int32, (LIN_CHUNK, LIN_CHUNK), 1)
    causal = ri >= ci
    strict = ri > ci

    for g0 in range(0, nc, GDN_CHUNK_GROUP):
        rows = [slice(c * LIN_CHUNK, (c + 1) * LIN_CHUNK) for c in range(g0, g0 + GDN_CHUNK_GROUP)]
        kc = [kn_s[r, :] for r in rows]
        gc = [gc_s[r, :] for r in rows]
        beta = [bt_s[r, :] for r in rows]
        decay = []
        for a in gc:
            diff = a[:, 0:LIN_CHUNK] - a.T[0:LIN_CHUNK, :]
            decay.append(jnp.where(causal, jnp.exp(jnp.where(causal, diff, 0.0)), 0.0))
        kb = [a * b for a, b in zip(kc, beta)]
        kcb = [a.astype(BF16) for a in kc]
        kk = [_dot_nt(a.astype(BF16), b) for a, b in zip(kb, kcb)]
        qk = [_dot_nt(qn_s[r, :].astype(BF16), b) for r, b in zip(rows, kcb)]
        lows = [jnp.where(strict, a * d, 0.0) for a, d in zip(kk, decay)]
        for r, a, d in zip(rows, qk, decay):
            at_s[r, :] = (a * d).astype(BF16)
        xinv = [a.astype(BF16) for a in _unit_lower_inverse_minus_eye(lows)]
        egc = [jnp.exp(a) for a in gc]
        vb = [vn_s[r, :] * b for r, b in zip(rows, beta)]
        kbe = [a * e for a, e in zip(kb, egc)]
        xu = [_dot(x, a.astype(BF16)) for x, a in zip(xinv, vb)]
        xw = [_dot(x, a.astype(BF16)) for x, a in zip(xinv, kbe)]
        for i, r in enumerate(rows):
            c = g0 + i
            u_s[r, :] = vb[i] + xu[i]
            w_s[r, :] = (kbe[i] + xw[i]).astype(BF16)
            qd_s[r, :] = (qn_s[r, :] * egc[i]).astype(BF16)
            g_last = gc[i][LIN_CHUNK - 1:LIN_CHUNK, :]
            kd_s[r, :] = (kc[i] * jnp.exp(g_last - gc[i])).astype(BF16)
            egl_s[c * SUBLANES:(c + 1) * SUBLANES, :] = jnp.broadcast_to(jnp.exp(g_last), (SUBLANES, LANES))


def _gdn_scan_kernel(u_ref, w_ref, at_ref, qd_ref, kd_ref, egl_ref, z_ref, ng_ref, o_ref, st_ref, *, tt):
    nc = tt // LIN_CHUNK
    hd = GDN_HEAD_DIM

    @pl.when(pl.program_id(0) == 0)
    def _():
        st_ref[...] = jnp.zeros_like(st_ref)

    ng = ng_ref[...]

    def scan_body(c, carry):
        r0 = pl.multiple_of(c * LIN_CHUNK, LIN_CHUNK)
        rows = pl.ds(r0, LIN_CHUNK)
        e0 = pl.multiple_of(c * SUBLANES, SUBLANES)
        for h in range(GDN_HEADS):
            cols = slice(h * hd, (h + 1) * hd)
            st = st_ref[h]
            stb = st.astype(BF16)
            v_new = u_ref[rows, cols] - _dot(w_ref[rows, cols], stb)
            vnb = v_new.astype(BF16)
            o = _dot(qd_ref[rows, cols], stb) + _dot(at_ref[h, rows, :], vnb)
            st_ref[h] = st * egl_ref[h, pl.ds(e0, 1), :] + _dot_tn(kd_ref[rows, cols], vnb)
            o = o * lax.rsqrt(jnp.mean(o * o, -1, keepdims=True) + RMS_EPS) * ng
            o_ref[rows, cols] = (o * _silu(z_ref[rows, cols])).astype(o_ref.dtype)
        return carry

    lax.fori_loop(0, nc, scan_body, 0)


def _gdn(proj, bd, conv_w, a_log, dt_bias, norm_g, tt_local, tt_scan):
    s = proj.shape[0]
    nh = GDN_HEADS
    hd = GDN_HEAD_DIM
    tt = tt_local
    per_head = lambda a: jnp.broadcast_to(a.astype(F32)[:, None, None], (nh, 1, LANES))
    blk = lambda off: pl.BlockSpec((tt, hd), lambda h, i: (i, off + h))
    cw = lambda off: pl.BlockSpec((GDN_CONV, hd), lambda h, i: (0, off + h))
    scal = pl.BlockSpec((None, 1, LANES), lambda h, i: (h, 0, 0))
    head_blk = pl.BlockSpec((tt, hd), lambda h, i: (i, h))
    u, w, at, qd, kd, egl = pl.pallas_call(
        functools.partial(_gdn_local_kernel, tt=tt),
        grid=(nh, s // tt),
        in_specs=[blk(0), blk(nh), blk(2 * nh),
                  pl.BlockSpec((tt, LANES), lambda h, i: (i, 0)),
                  cw(0), cw(nh), cw(2 * nh), scal, scal],
        out_specs=[head_blk, head_blk,
                   pl.BlockSpec((None, tt, LIN_CHUNK), lambda h, i: (h, i, 0)),
                   head_blk, head_blk,
                   pl.BlockSpec((None, tt // LIN_CHUNK * SUBLANES, LANES), lambda h, i: (h, i, 0))],
        out_shape=[jax.ShapeDtypeStruct((s, nh * hd), F32), jax.ShapeDtypeStruct((s, nh * hd), BF16),
                   jax.ShapeDtypeStruct((nh, s, LIN_CHUNK), BF16),
                   jax.ShapeDtypeStruct((s, nh * hd), BF16), jax.ShapeDtypeStruct((s, nh * hd), BF16),
                   jax.ShapeDtypeStruct((nh, s // LIN_CHUNK * SUBLANES, LANES), F32)],
        scratch_shapes=[pltpu.VMEM((SUBLANES, hd), F32), pltpu.VMEM((SUBLANES, hd), F32),
                        pltpu.VMEM((SUBLANES, hd), F32),
                        pltpu.VMEM((tt, hd), F32), pltpu.VMEM((tt, hd), F32), pltpu.VMEM((tt, hd), F32),
                        pltpu.VMEM((tt, LANES), F32), pltpu.VMEM((tt, LANES), F32)],
        compiler_params=_params(("parallel", "arbitrary")),
        name="gdn_local",
    )(proj, proj, proj, bd, conv_w, conv_w, conv_w, per_head(a_log), per_head(dt_bias))
    tt = tt_scan
    wide = pl.BlockSpec((tt, nh * hd), lambda i: (i, 0))
    return pl.pallas_call(
        functools.partial(_gdn_scan_kernel, tt=tt),
        grid=(s // tt,),
        in_specs=[wide, wide,
                  pl.BlockSpec((nh, tt, LIN_CHUNK), lambda i: (0, i, 0)),
                  wide, wide,
                  pl.BlockSpec((nh, tt // LIN_CHUNK * SUBLANES, LANES), lambda i: (0, i, 0)),
                  pl.BlockSpec((tt, nh * hd), lambda i: (i, 3)),
                  pl.BlockSpec((1, hd), lambda i: (0, 0))],
        out_specs=wide,
        out_shape=jax.ShapeDtypeStruct((s, nh * hd), BF16),
        scratch_shapes=[pltpu.VMEM((nh, hd, hd), F32)],
        compiler_params=_params(("arbitrary",)),
        name="gdn_scan",
    )(u, w, at, qd, kd, egl, proj, norm_g.reshape(1, hd))


S5_LANE_CHUNK = 512
S5_GROUPS_PER_TILE = LANES // S5_GROUP_CH


def _s5_kernel(u_ref, bre_ref, bim_ref, cre_ref, cim_ref, are_ref, aim_ref, pre_ref, pim_ref, d_ref,
               o_ref, nat_s, up_s, xr_s, xi_s, cr_s, ci_s, gr_s, gi_s, *, tt):
    nseg = SUBLANES
    seg = tt // nseg
    nstate = S5_GROUPS * S5_STATE
    ntile = S5_CHANNELS // LANES
    spt = S5_GROUPS_PER_TILE * S5_STATE

    @pl.when(pl.program_id(0) == 0)
    def _():
        gr_s[...] = jnp.zeros_like(gr_s)
        gi_s[...] = jnp.zeros_like(gi_s)

    for t in range(ntile):
        nat_s[t] = u_ref[:, t * LANES:(t + 1) * LANES]
    for t in range(ntile):
        for j in range(seg):
            up_s[t, j * nseg:(j + 1) * nseg, :] = nat_s[t, pl.ds(j, nseg, stride=seg), :]
    for t in range(ntile):
        ub = up_s[t].astype(BF16)
        xr_s[:, t * spt:(t + 1) * spt] = _dot(ub, bre_ref[t])
        xi_s[:, t * spt:(t + 1) * spt] = _dot(ub, bim_ref[t])

    for lc in range(nstate // S5_LANE_CHUNK):
        cols = slice(lc * S5_LANE_CHUNK, (lc + 1) * S5_LANE_CHUNK)
        ar = are_ref[:, cols]
        ai = aim_ref[:, cols]

        def local_step(j, carry):
            xr, xi = carry
            r0 = pl.multiple_of(j * nseg, nseg)
            nr = ar * xr - ai * xi + xr_s[pl.ds(r0, nseg), cols]
            ni = ar * xi + ai * xr + xi_s[pl.ds(r0, nseg), cols]
            xr_s[pl.ds(r0, nseg), cols] = nr
            xi_s[pl.ds(r0, nseg), cols] = ni
            return nr, ni

        zero = jnp.zeros((nseg, S5_LANE_CHUNK), F32)
        er, ei = lax.fori_loop(0, seg, local_step, (zero, zero))
        pr = pre_ref[seg - 1:seg, cols]
        pi = pim_ref[seg - 1:seg, cols]
        cr = gr_s[:, cols]
        ci = gi_s[:, cols]
        for r in range(nseg):
            cr_s[r:r + 1, cols] = cr
            ci_s[r:r + 1, cols] = ci
            cr, ci = (pr * cr - pi * ci + er[r:r + 1], pr * ci + pi * cr + ei[r:r + 1])
        gr_s[:, cols] = cr
        gi_s[:, cols] = ci
        c_re = cr_s[:, cols]
        c_im = ci_s[:, cols]

        def fix_step(j, carry):
            r0 = pl.multiple_of(j * nseg, nseg)
            pjr = pre_ref[pl.ds(j, 1), cols]
            pji = pim_ref[pl.ds(j, 1), cols]
            xr_s[pl.ds(r0, nseg), cols] = xr_s[pl.ds(r0, nseg), cols] + pjr * c_re - pji * c_im
            xi_s[pl.ds(r0, nseg), cols] = xi_s[pl.ds(r0, nseg), cols] + pjr * c_im + pji * c_re
            return carry

        lax.fori_loop(0, seg, fix_step, 0)

    for t in range(ntile):
        xr = xr_s[:, t * spt:(t + 1) * spt].astype(BF16)
        xi = xi_s[:, t * spt:(t + 1) * spt].astype(BF16)
        y = _dot(xr, cre_ref[t]) - _dot(xi, cim_ref[t]) + d_ref[:, t * LANES:(t + 1) * LANES] * up_s[t]
        up_s[t] = 0.5 * y * (1.0 + jnp.tanh(math.sqrt(2.0 / math.pi) * (y + 0.044715 * (y * y * y))))
    for t in range(ntile):
        for j in range(seg):
            nat_s[t, pl.ds(j, nseg, stride=seg), :] = up_s[t, j * nseg:(j + 1) * nseg, :]
    for t in range(ntile):
        o_ref[:, t * LANES:(t + 1) * LANES] = nat_s[t]


def _s5(proj, col0, a_re, a_im, b_re, b_im, c_re, c_im, d, log_step, tt):
    s = proj.shape[0]
    seg = tt // SUBLANES
    nstate = S5_GROUPS * S5_STATE
    ntile = S5_CHANNELS // LANES
    gpt = S5_GROUPS_PER_TILE
    a_re, a_im, b_re, b_im, c_re, c_im, d = (t.astype(F32) for t in (a_re, a_im, b_re, b_im, c_re, c_im, d))
    step = jnp.exp(log_step.astype(F32))[:, None]
    mag = jnp.exp(a_re * step)
    ab_re = mag * jnp.cos(a_im * step)
    ab_im = mag * jnp.sin(a_im * step)
    den = jnp.square(a_re) + jnp.square(a_im)
    z_re = ((ab_re - 1.0) * a_re + ab_im * a_im) / den
    z_im = (ab_im * a_re - (ab_re - 1.0) * a_im) / den
    bb_re = z_re[..., None] * b_re - z_im[..., None] * b_im
    bb_im = z_re[..., None] * b_im + z_im[..., None] * b_re

    def block_diag_in(bb):
        bt = bb.reshape(ntile, gpt, S5_STATE, S5_GROUP_CH)
        eye = jnp.eye(gpt, dtype=F32)
        m = jnp.einsum('tgpc,gh->tgchp', bt, eye)
        return m.reshape(ntile, gpt * S5_GROUP_CH, gpt * S5_STATE).astype(BF16)

    def block_diag_out(cc):
        ct = cc.reshape(ntile, gpt, S5_GROUP_CH, S5_STATE)
        eye = jnp.eye(gpt, dtype=F32)
        m = jnp.einsum('tgcp,gh->tgphc', ct, eye)
        return m.reshape(ntile, gpt * S5_STATE, gpt * S5_GROUP_CH).astype(BF16)

    def pow_step(carry, _):
        pr, pi = carry
        nr, ni = pr * ab_re - pi * ab_im, pr * ab_im + pi * ab_re
        return (nr, ni), (pr, pi)

    _, (pw_re, pw_im) = lax.scan(pow_step, (ab_re, ab_im), None, length=seg)
    pw_re = pw_re.reshape(seg, nstate)
    pw_im = pw_im.reshape(seg, nstate)
    full = lambda shape: pl.BlockSpec(shape, lambda i: (0,) * len(shape))
    return pl.pallas_call(
        functools.partial(_s5_kernel, tt=tt),
        grid=(s // tt,),
        in_specs=[pl.BlockSpec((tt, S5_CHANNELS), lambda i: (i, col0)),
                  full((ntile, LANES, gpt * S5_STATE)), full((ntile, LANES, gpt * S5_STATE)),
                  full((ntile, gpt * S5_STATE, LANES)), full((ntile, gpt * S5_STATE, LANES)),
                  full((1, nstate)), full((1, nstate)),
                  full((seg, nstate)), full((seg, nstate)),
                  full((1, S5_CHANNELS))],
        out_specs=pl.BlockSpec((tt, S5_CHANNELS), lambda i: (i, 0)),
        out_shape=jax.ShapeDtypeStruct((s, S5_CHANNELS), F32),
        scratch_shapes=[pltpu.VMEM((ntile, tt, LANES), F32), pltpu.VMEM((ntile, tt, LANES), F32),
                        pltpu.VMEM((tt, nstate), F32), pltpu.VMEM((tt, nstate), F32),
                        pltpu.VMEM((SUBLANES, nstate), F32), pltpu.VMEM((SUBLANES, nstate), F32),
                        pltpu.VMEM((1, nstate), F32), pltpu.VMEM((1, nstate), F32)],
        compiler_params=_params(("arbitrary",)),
        name="s5",
    )(proj, block_diag_in(bb_re), block_diag_in(bb_im), block_diag_out(c_re), block_diag_out(c_im),
      ab_re.reshape(1, nstate), ab_im.reshape(1, nstate), pw_re, pw_im, d.reshape(1, S5_CHANNELS))


def _glu_kernel(y_ref, w_ref, b_ref, o_ref):
    y = y_ref[...]
    t = _dot(y.astype(BF16), w_ref[...]) + b_ref[...]
    o_ref[...] = (y * _sigmoid(t)).astype(o_ref.dtype)


def _glu(y, w, b, tm):
    m, n = y.shape
    return pl.pallas_call(
        _glu_kernel,
        grid=(m // tm,),
        in_specs=[pl.BlockSpec((tm, n), lambda i: (i, 0)),
                  pl.BlockSpec((n, n), lambda i: (0, 0)),
                  pl.BlockSpec((1, n), lambda i: (0, 0))],
        out_specs=pl.BlockSpec((tm, n), lambda i: (i, 0)),
        out_shape=jax.ShapeDtypeStruct((m, n), BF16),
        compiler_params=_params(("parallel",)),
        name="glu",
    )(y, w, b.reshape(1, n))


def _pad_cols(w, n):
    return jnp.concatenate([w, jnp.zeros((w.shape[0], n - w.shape[1]), w.dtype)], axis=1)


def _split_cols(w, sizes):
    offs = [0]
    for sz in sizes:
        offs.append(offs[-1] + sz)
    return [w[:, offs[i]:offs[i + 1]] for i in range(len(sizes))]


def _even_mixer(xb, w_in, w_gate2, b_gate, norm_g):
    hk, hv, hm = GLA_HEADS * GLA_DK, GLA_HEADS * GLA_DV, MOBA_HEADS * MOBA_HEAD_DIM
    gq, gk, gv, g_lr, gr, mq, mk, mv = _split_cols(w_in, (hk, hk, hv, GLA_GATE_RANK, hv, hm, hm, hm))
    w_main = jnp.concatenate([gq, gk, gv, gr, mq, mk, mv], axis=1).astype(BF16)
    proj = _matmul(xb, w_main, F32, 1024, 1024)
    glr = _matmul(xb, _pad_cols(g_lr, LANES).astype(BF16), F32, 1024, LANES)
    gla_out = _gla(proj, glr, w_gate2, b_gate, norm_g, 512)
    moba_out = _moba(proj, (2 * hk + 2 * hv) // LANES)
    return jnp.concatenate([gla_out, moba_out], axis=-1)


def _odd_mixer(xb, w_in, conv_w, a_log, dt_bias, norm_g, a_re, a_im, b_re, b_im, c_re, c_im, d, log_step,
               glu_w, glu_b):
    hq = GDN_HEADS * GDN_HEAD_DIM
    qkv, z, beta, decay, u = _split_cols(w_in, (3 * hq, hq, GDN_HEADS, GDN_HEADS, S5_CHANNELS))
    w_main = jnp.concatenate([qkv, z, u], axis=1).astype(BF16)
    proj = _matmul(xb, w_main, F32, 1024, 1024)
    bd = _matmul(xb, _pad_cols(jnp.concatenate([beta, decay], axis=1), LANES).astype(BF16), F32, 1024, LANES)
    gdn_out = _gdn(proj, bd, conv_w, a_log, dt_bias, norm_g, 512, 512)
    y = _s5(proj, 4 * hq // S5_CHANNELS, a_re, a_im, b_re, b_im, c_re, c_im, d, log_step, 512)
    s5_out = _glu(y, glu_w.astype(BF16), glu_b, 512)
    return jnp.concatenate([gdn_out, s5_out], axis=-1)


def kernel(x, even_w_in, gla_w_gate2, gla_b_gate, gla_norm_g, even_w_out, odd_w_in, gdn_conv_w, gdn_a_log,
           gdn_dt_bias, gdn_norm_g, s5_a_re, s5_a_im, s5_b_re, s5_b_im, s5_c_re, s5_c_im, s5_d, s5_log_step,
           s5_glu_w, s5_glu_b, odd_w_out, ln_mix_g, ln_mix_b, ffn_w_up, ffn_conv_w, ffn_w_down, ln_ffn_g, ln_ffn_b):
    bsz, seq, dm = x.shape
    outs = []
    for b in range(bsz):
        xf = x[b]
        xb = xf.astype(BF16)
        for i in range(DEPTH):
            j = i // 2
            if i % 2 == 0:
                mix_in = _even_mixer(xb, even_w_in[j], gla_w_gate2[j], gla_b_gate[j], gla_norm_g[j])
                w_out = even_w_out[j]
            else:
                mix_in = _odd_mixer(xb, odd_w_in[j], gdn_conv_w[j], gdn_a_log[j], gdn_dt_bias[j], gdn_norm_g[j],
                                    s5_a_re[j], s5_a_im[j], s5_b_re[j], s5_b_im[j], s5_c_re[j], s5_c_im[j],
                                    s5_d[j], s5_log_step[j], s5_glu_w[j], s5_glu_b[j])
                w_out = odd_w_out[j]
            xf, xb = _matmul_res_ln(mix_in, w_out.astype(BF16), xf, ln_mix_g[i], ln_mix_b[i], 512, dm)
            mid = _ffn_up(xb, ffn_w_up[i].astype(BF16), ffn_conv_w[i], 1024, 512)
            xf, xb = _matmul_res_ln(mid, ffn_w_down[i].astype(BF16), xf, ln_ffn_g[i], ln_ffn_b[i], 512, D_FF // 4)
        outs.append(xf)
    return jnp.stack(outs, axis=0)
```

```python
import functools
import math

import jax
import jax.numpy as jnp
from jax import lax
from jax.experimental import pallas as pl
from jax.experimental.pallas import tpu as pltpu

F32 = jnp.float32
BF16 = jnp.bfloat16
HIGHEST = lax.Precision.HIGHEST

DEPTH = 4
GLA_HEADS, GLA_DK, GLA_DV, GLA_GATE_RANK, GLA_GATE_NORM = 8, 64, 128, 16, 16.0
MOBA_HEADS, MOBA_HEAD_DIM, MOBA_BLOCK, MOBA_TOPK = 8, 128, 256, 3
GDN_HEADS, GDN_HEAD_DIM, GDN_CONV = 8, 128, 4
S5_CHANNELS, S5_GROUP_CH, S5_STATE = 1024, 16, 64
S5_GROUPS = S5_CHANNELS // S5_GROUP_CH
LIN_CHUNK = 64
D_FF = 5632
FFN_CONV = 3
DEEPNORM_ALPHA = (2 * DEPTH) ** 0.25
LN_EPS = 1e-5
RMS_EPS = 1e-6

LANES = 128
SUBLANES = 8
VMEM_LIMIT_BYTES = 56 * 1024 * 1024
NEG_BIG = -1e30


def _params(semantics):
    return pltpu.CompilerParams(dimension_semantics=semantics, vmem_limit_bytes=VMEM_LIMIT_BYTES)


def _dot(a, b, precision=None):
    return jnp.dot(a, b, preferred_element_type=F32, precision=precision)


def _dot_nt(a, b, precision=None):
    return lax.dot_general(a, b, (((1,), (1,)), ((), ())), preferred_element_type=F32, precision=precision)


def _dot_tn(a, b, precision=None):
    return lax.dot_general(a, b, (((0,), (0,)), ((), ())), preferred_element_type=F32, precision=precision)


def _sigmoid(x):
    return 1.0 / (1.0 + jnp.exp(-x))


def _silu(x):
    return x * _sigmoid(x)


def _chunk_cumsum(x, chunk):
    pos = lax.broadcasted_iota(jnp.int32, x.shape, 0) % chunk
    k = 1
    while k < chunk:
        x = x + jnp.where(pos >= k, pltpu.roll(x, k, 0), 0.0)
        k *= 2
    return x


def _mm_kernel(a_ref, b_ref, o_ref):
    o_ref[...] = _dot(a_ref[...].astype(BF16), b_ref[...]).astype(o_ref.dtype)


def _matmul(a, b, out_dtype, tm, tn):
    m, k = a.shape
    n = b.shape[1]
    return pl.pallas_call(
        _mm_kernel,
        grid=(m // tm, n // tn),
        in_specs=[pl.BlockSpec((tm, k), lambda i, j: (i, 0)),
                  pl.BlockSpec((k, tn), lambda i, j: (0, j))],
        out_specs=pl.BlockSpec((tm, tn), lambda i, j: (i, j)),
        out_shape=jax.ShapeDtypeStruct((m, n), out_dtype),
        compiler_params=_params(("parallel", "parallel")),
        name="matmul",
    )(a, b)


def _mm_ln_kernel(a_ref, b_ref, x_ref, g_ref, bt_ref, of_ref, ob_ref, acc_ref, *, nk):
    k = pl.program_id(1)

    @pl.when(k == 0)
    def _():
        acc_ref[...] = jnp.zeros_like(acc_ref)

    acc_ref[...] += _dot(a_ref[...], b_ref[...])

    @pl.when(k == nk - 1)
    def _():
        y = DEEPNORM_ALPHA * x_ref[...] + acc_ref[...]
        mu = jnp.mean(y, -1, keepdims=True)
        d = y - mu
        var = jnp.mean(d * d, -1, keepdims=True)
        o = d * lax.rsqrt(var + LN_EPS) * g_ref[...] + bt_ref[...]
        of_ref[...] = o
        ob_ref[...] = o.astype(BF16)


def _matmul_res_ln(a, b, x, g, bt, tm, tk):
    m, k = a.shape
    n = b.shape[1]
    nk = k // tk
    return pl.pallas_call(
        functools.partial(_mm_ln_kernel, nk=nk),
        grid=(m // tm, nk),
        in_specs=[pl.BlockSpec((tm, tk), lambda i, kk: (i, kk)),
                  pl.BlockSpec((tk, n), lambda i, kk: (kk, 0)),
                  pl.BlockSpec((tm, n), lambda i, kk: (i, 0)),
                  pl.BlockSpec((1, n), lambda i, kk: (0, 0)),
                  pl.BlockSpec((1, n), lambda i, kk: (0, 0))],
        out_specs=[pl.BlockSpec((tm, n), lambda i, kk: (i, 0)),
                   pl.BlockSpec((tm, n), lambda i, kk: (i, 0))],
        out_shape=[jax.ShapeDtypeStruct((m, n), F32), jax.ShapeDtypeStruct((m, n), BF16)],
        scratch_shapes=[pltpu.VMEM((tm, n), F32)],
        compiler_params=_params(("parallel", "arbitrary")),
        name="matmul_res_ln",
    )(a, b, x, g.reshape(1, n), bt.reshape(1, n))


def _causal_conv_rows(h, w, prev):
    width = w.shape[0]
    out = w[width - 1:width] * h
    top = jnp.concatenate([prev, h[0:SUBLANES]], axis=0)
    out_top = w[width - 1:width] * top
    for s in range(1, width):
        wj = w[width - 1 - s:width - s]
        out = out + wj * pltpu.roll(h, s, 0)
        out_top = out_top + wj * pltpu.roll(top, s, 0)
    return out, out_top[SUBLANES:]


def _ffn_up_kernel(x_ref, wg_ref, wv_ref, cg_ref, cv_ref, o_ref, carry_g, carry_v, wgb_s, wvb_s):
    @pl.when(pl.program_id(1) == 0)
    def _():
        carry_g[...] = jnp.zeros_like(carry_g)
        carry_v[...] = jnp.zeros_like(carry_v)
        wgb_s[...] = wg_ref[...].astype(BF16)
        wvb_s[...] = wv_ref[...].astype(BF16)

    x = x_ref[...]
    tm = x.shape[0]
    hg = _dot(x, wgb_s[...])
    hv = _dot(x, wvb_s[...])
    g, g_top = _causal_conv_rows(hg, cg_ref[...], carry_g[...])
    v, v_top = _causal_conv_rows(hv, cv_ref[...], carry_v[...])
    carry_g[...] = hg[tm - SUBLANES:]
    carry_v[...] = hv[tm - SUBLANES:]
    o_ref[...] = (_silu(g) * v).astype(o_ref.dtype)
    o_ref[0:SUBLANES, :] = (_silu(g_top) * v_top).astype(o_ref.dtype)


def _ffn_up(xb, w_up, conv_w, layer, tm, tn):
    m, k = xb.shape
    d_ff = w_up.shape[2] // 2
    nt = d_ff // tn
    return pl.pallas_call(
        _ffn_up_kernel,
        grid=(nt, m // tm),
        in_specs=[pl.BlockSpec((tm, k), lambda j, i: (i, 0)),
                  pl.BlockSpec((None, k, tn), lambda j, i: (layer, 0, j)),
                  pl.BlockSpec((None, k, tn), lambda j, i: (layer, 0, j + nt)),
                  pl.BlockSpec((None, FFN_CONV, tn), lambda j, i: (layer, 0, j)),
                  pl.BlockSpec((None, FFN_CONV, tn), lambda j, i: (layer, 0, j + nt))],
        out_specs=pl.BlockSpec((tm, tn), lambda j, i: (i, j)),
        out_shape=jax.ShapeDtypeStruct((m, d_ff), BF16),
        scratch_shapes=[pltpu.VMEM((SUBLANES, tn), F32), pltpu.VMEM((SUBLANES, tn), F32),
                        pltpu.VMEM((k, tn), BF16), pltpu.VMEM((k, tn), BF16)],
        compiler_params=_params(("parallel", "arbitrary")),
        name="ffn_up",
    )(xb, w_up, w_up, conv_w, conv_w)


def _gla_kernel(q_ref, k_ref, v_ref, r_ref, glr_ref, wg2_ref, bg_ref, ng_ref, o_ref,
                st_ref, qe_s, ke_s, kd_s, vb_s, ecl_s, *, tt):
    nc = tt // LIN_CHUNK
    hk = GLA_HEADS * GLA_DK

    @pl.when(pl.program_id(0) == 0)
    def _():
        st_ref[...] = jnp.zeros_like(st_ref)

    z = _dot(glr_ref[...], wg2_ref[...], HIGHEST) + bg_ref[...]
    log_a = (jnp.minimum(z, 0.0) - jnp.log(1.0 + jnp.exp(-jnp.abs(z)))) / GLA_GATE_NORM
    cum = _chunk_cumsum(log_a, LIN_CHUNK)
    q = q_ref[...] * (GLA_DK ** -0.5)
    k = k_ref[...]
    qe_s[...] = (q * jnp.exp(cum)).astype(BF16)
    ke_s[...] = (k * jnp.exp(-cum)).astype(BF16)
    for c in range(nc):
        rows = slice(c * LIN_CHUNK, (c + 1) * LIN_CHUNK)
        cl = cum[(c + 1) * LIN_CHUNK - 1:(c + 1) * LIN_CHUNK, :]
        kd_s[rows, :] = (k[rows] * jnp.exp(cl - cum[rows])).astype(BF16)
        ecl_s[c * SUBLANES:(c + 1) * SUBLANES, :] = jnp.broadcast_to(jnp.exp(cl), (SUBLANES, hk))
    vb_s[...] = v_ref[...].astype(BF16)

    ri = lax.broadcasted_iota(jnp.int32, (LIN_CHUNK, LIN_CHUNK), 0)
    ci = lax.broadcasted_iota(jnp.int32, (LIN_CHUNK, LIN_CHUNK), 1)
    causal = ri >= ci
    lane = lax.broadcasted_iota(jnp.int32, (LIN_CHUNK, LANES), 1)
    lo_half = lane < GLA_DK
    ng = ng_ref[...]

    def chunk_body(c, carry):
        r0 = pl.multiple_of(c * LIN_CHUNK, LIN_CHUNK)
        e0 = pl.multiple_of(c * SUBLANES, SUBLANES)
        for h in range(GLA_HEADS):
            pcol = slice((h // 2) * LANES, (h // 2 + 1) * LANES)
            keep = lo_half if h % 2 == 0 else jnp.logical_not(lo_half)
            qm = jnp.where(keep, qe_s[pl.ds(r0, LIN_CHUNK), pcol], 0)
            kdm = jnp.where(keep, kd_s[pl.ds(r0, LIN_CHUNK), pcol], 0)
            ke = ke_s[pl.ds(r0, LIN_CHUNK), pcol]
            vh = vb_s[pl.ds(r0, LIN_CHUNK), h * GLA_DV:(h + 1) * GLA_DV]
            st = st_ref[h]
            attn = jnp.where(causal, _dot_nt(qm, ke), 0.0)
            o = _dot(attn.astype(BF16), vh) + _dot_nt(qm, st.astype(BF16))
            st_ref[h] = st * ecl_s[pl.ds(e0, 1), pcol] + _dot_tn(vh, kdm)
            o = o * lax.rsqrt(jnp.mean(o * o, -1, keepdims=True) + RMS_EPS) * ng
            gate = r_ref[pl.ds(r0, LIN_CHUNK), h * GLA_DV:(h + 1) * GLA_DV]
            o_ref[pl.ds(r0, LIN_CHUNK), h * GLA_DV:(h + 1) * GLA_DV] = (o * _silu(gate)).astype(o_ref.dtype)
        return carry

    lax.fori_loop(0, nc, chunk_body, 0)


def _gla(proj, glr, w_gate2, b_gate, norm_g, tt):
    s = proj.shape[0]
    hk = GLA_HEADS * GLA_DK
    hv = GLA_HEADS * GLA_DV
    wg2 = jnp.zeros((LANES, hk), F32).at[:GLA_GATE_RANK].set(w_gate2)
    return pl.pallas_call(
        functools.partial(_gla_kernel, tt=tt),
        grid=(s // tt,),
        in_specs=[pl.BlockSpec((tt, hk), lambda i: (i, 0)),
                  pl.BlockSpec((tt, hk), lambda i: (i, 1)),
                  pl.BlockSpec((tt, hv), lambda i: (i, 1)),
                  pl.BlockSpec((tt, hv), lambda i: (i, 2)),
                  pl.BlockSpec((tt, LANES), lambda i: (i, 0)),
                  pl.BlockSpec((LANES, hk), lambda i: (0, 0)),
                  pl.BlockSpec((1, hk), lambda i: (0, 0)),
                  pl.BlockSpec((1, GLA_DV), lambda i: (0, 0))],
        out_specs=pl.BlockSpec((tt, hv), lambda i: (i, 0)),
        out_shape=jax.ShapeDtypeStruct((s, hv), BF16),
        scratch_shapes=[pltpu.VMEM((GLA_HEADS, GLA_DV, LANES), F32),
                        pltpu.VMEM((tt, hk), BF16), pltpu.VMEM((tt, hk), BF16), pltpu.VMEM((tt, hk), BF16),
                        pltpu.VMEM((tt, hv), BF16),
                        pltpu.VMEM((tt // LIN_CHUNK * SUBLANES, hk), F32)],
        compiler_params=_params(("arbitrary",)),
        name="gla",
    )(proj, proj, proj, proj, glr, wg2, b_gate.reshape(1, hk), norm_g.reshape(1, GLA_DV))


def _split3(x):
    hi = x.astype(BF16).astype(F32)
    rest = x - hi
    mid = rest.astype(BF16).astype(F32)
    lo = (rest - mid).astype(BF16).astype(F32)
    return hi, mid, lo


MOBA_SEL_LANES = 32
MOBA_ROW_LANE = 96
MOBA_COL_LANE = 99
LOG2E = 1.4426950408889634
MOBA_GROUP = 4


def _moba_kernel(q_ref, k_ref, v_ref, slope_ref, o_ref, kmean_s, kx_s, vt_s, vtb_s, qxt_s, qxo_s, s_s, so_s, acc_s,
                 *, n_blk):
    blk = MOBA_BLOCK
    hd = MOBA_HEAD_DIM
    i = pl.program_id(1)
    sl2 = slope_ref[...][:, 0:1] * LOG2E
    s_hi, s_mid, s_lo = _split3(sl2)

    @pl.when(i == 0)
    def _():
        lane = lax.broadcasted_iota(jnp.int32, (blk, LANES), 1)
        key_f = lax.broadcasted_iota(jnp.int32, (blk, LANES), 0).astype(F32)
        k_extra = jnp.where(lane == MOBA_ROW_LANE, -s_hi, jnp.where(lane == MOBA_ROW_LANE + 1, -s_mid,
                  jnp.where(lane == MOBA_ROW_LANE + 2, -s_lo,
                  jnp.where(jnp.logical_and(lane >= MOBA_COL_LANE, lane < MOBA_COL_LANE + 3), key_f, 0.0))))
        kmean_s[...] = jnp.zeros_like(kmean_s)
        for n in range(n_blk):
            rows = slice(n * blk, (n + 1) * blk)
            kb = k_ref[rows, :]
            kmean_s[n:n + 1, :] = jnp.mean(kb, axis=0, keepdims=True)
            onehot = jnp.logical_or(lane == n, jnp.logical_or(lane == MOBA_SEL_LANES + n,
                                                               lane == 2 * MOBA_SEL_LANES + n))
            kx_s[rows, 0:hd] = kb.astype(BF16)
            kx_s[rows, hd:] = jnp.where(onehot, 1.0, k_extra).astype(BF16)
            gcols = slice((n % MOBA_GROUP) * blk, (n % MOBA_GROUP + 1) * blk)
            vt = v_ref[rows, :].T.astype(BF16)
            vt_s[n // MOBA_GROUP, :, gcols] = vt
            vtb_s[n] = vt

    q = q_ref[...]
    nb = lax.broadcasted_iota(jnp.int32, (MOBA_SEL_LANES, blk), 0)
    nb_f = nb.astype(F32)
    gate = _dot_nt(kmean_s[...], q, HIGHEST)
    gate = jnp.where(nb < i, gate, -jnp.inf)
    chosen = jnp.zeros((MOBA_SEL_LANES, blk), jnp.bool_)
    for _ in range(MOBA_TOPK):
        best = jnp.max(gate, axis=0, keepdims=True)
        idx = jnp.min(jnp.where(gate == best, nb_f, float(MOBA_SEL_LANES)), axis=0, keepdims=True)
        hit = nb_f == idx
        chosen = jnp.logical_or(chosen, hit)
        gate = jnp.where(hit, -jnp.inf, gate)
    sel = jnp.where(jnp.logical_and(chosen, nb < i), -sl2 * ((i - nb) * blk).astype(F32), NEG_BIG)
    b_hi, b_mid, b_lo = _split3(sel)
    qry_f = lax.broadcasted_iota(jnp.int32, (MOBA_SEL_LANES, blk), 1).astype(F32)
    tail = jnp.where(nb < 3, qry_f, jnp.where(nb == 3, s_hi, jnp.where(nb == 4, s_mid,
           jnp.where(nb == 5, s_lo, 0.0))))
    qt = (q * (hd ** -0.5 * LOG2E)).T.astype(BF16)
    qxt_s[0:hd, :] = qt
    qxt_s[hd:, :] = jnp.concatenate([b_hi, b_mid, b_lo, tail], axis=0).astype(BF16)
    qxo_s[0:hd, :] = qt
    qxo_s[hd:, :] = jnp.concatenate([jnp.zeros((3 * MOBA_SEL_LANES, blk), F32), tail], axis=0).astype(BF16)

    def fold(x):
        return x.reshape(x.shape[0] // SUBLANES, SUBLANES, blk)

    key = lax.broadcasted_iota(jnp.int32, (blk, blk), 0)
    qry = lax.broadcasted_iota(jnp.int32, (blk, blk), 1)
    own0 = pl.multiple_of(i * blk, blk)
    s_own = jnp.where(key <= qry, _dot(kx_s[pl.ds(own0, blk), :], qxo_s[...]), NEG_BIG)
    so_s[...] = s_own

    ngrp = (i + MOBA_GROUP - 1) // MOBA_GROUP
    gkeys = MOBA_GROUP * blk

    def score_group(g, mx):
        k0 = pl.multiple_of(g * gkeys, gkeys)
        s = _dot(kx_s[pl.ds(k0, gkeys), :], qxt_s[...])
        s_s[pl.ds(g * MOBA_GROUP, MOBA_GROUP)] = s.reshape(MOBA_GROUP, blk, blk)
        return jnp.maximum(mx, jnp.max(fold(s), axis=0))

    mx = lax.fori_loop(0, ngrp, score_group, jnp.max(fold(s_own), axis=0))
    m_row = jnp.max(mx, axis=0, keepdims=True)

    p_own = jnp.exp2(so_s[...] - m_row)
    acc_s[...] = _dot(vtb_s[i], p_own.astype(BF16))

    def prob_group(g, ls):
        s = s_s[pl.ds(g * MOBA_GROUP, MOBA_GROUP)].reshape(gkeys, blk)
        p = jnp.exp2(s - m_row)
        acc_s[...] += _dot(vt_s[g], p.astype(BF16))
        return ls + jnp.sum(fold(p), axis=0)

    ls = lax.fori_loop(0, ngrp, prob_group, jnp.sum(fold(p_own), axis=0))
    l_row = jnp.sum(ls, axis=0, keepdims=True)
    o_ref[...] = (acc_s[...] / l_row).T.astype(o_ref.dtype)


def _moba(proj, col0):
    s = proj.shape[0]
    n_blk = s // MOBA_BLOCK
    assert n_blk <= MOBA_SEL_LANES
    slopes = jnp.exp2(-8.0 * jnp.arange(1, MOBA_HEADS + 1, dtype=F32) / MOBA_HEADS)
    slopes = jnp.broadcast_to(slopes[:, None, None], (MOBA_HEADS, 1, LANES))
    return pl.pallas_call(
        functools.partial(_moba_kernel, n_blk=n_blk),
        grid=(MOBA_HEADS, n_blk),
        in_specs=[pl.BlockSpec((MOBA_BLOCK, MOBA_HEAD_DIM), lambda h, i: (i, col0 + h)),
                  pl.BlockSpec((s, MOBA_HEAD_DIM), lambda h, i: (0, col0 + MOBA_HEADS + h)),
                  pl.BlockSpec((s, MOBA_HEAD_DIM), lambda h, i: (0, col0 + 2 * MOBA_HEADS + h)),
                  pl.BlockSpec((None, 1, LANES), lambda h, i: (h, 0, 0))],
        out_specs=pl.BlockSpec((MOBA_BLOCK, MOBA_HEAD_DIM), lambda h, i: (i, h)),
        out_shape=jax.ShapeDtypeStruct((s, MOBA_HEADS * MOBA_HEAD_DIM), BF16),
        scratch_shapes=[pltpu.VMEM((MOBA_SEL_LANES, MOBA_HEAD_DIM), F32),
                        pltpu.VMEM((s, MOBA_HEAD_DIM + LANES), BF16),
                        pltpu.VMEM((n_blk // MOBA_GROUP, MOBA_HEAD_DIM, MOBA_GROUP * MOBA_BLOCK), BF16),
                        pltpu.VMEM((n_blk, MOBA_HEAD_DIM, MOBA_BLOCK), BF16),
                        pltpu.VMEM((MOBA_HEAD_DIM + LANES, MOBA_BLOCK), BF16),
                        pltpu.VMEM((MOBA_HEAD_DIM + LANES, MOBA_BLOCK), BF16),
                        pltpu.VMEM((n_blk, MOBA_BLOCK, MOBA_BLOCK), F32),
                        pltpu.VMEM((MOBA_BLOCK, MOBA_BLOCK), F32),
                        pltpu.VMEM((MOBA_HEAD_DIM, MOBA_BLOCK), F32)],
        compiler_params=_params(("parallel", "arbitrary")),
        name="moba",
    )(proj, proj, proj, slopes)


GDN_CHUNK_GROUP = 8


def _unit_lower_inverse_minus_eye(lows):
    n = lows[0].shape[0]
    r = lax.broadcasted_iota(jnp.int32, (n, n), 0)
    c = lax.broadcasted_iota(jnp.int32, (n, n), 1)
    eye = jnp.where(r == c, 1.0, 0.0)
    same16 = (r // 16) == (c // 16)
    same32 = (r // 32) == (c // 32)
    bdot = lambda a, b: [_dot(x.astype(BF16), y.astype(BF16)) for x, y in zip(a, b)]
    ld = [jnp.where(same16, m, 0.0) for m in lows]
    p2 = bdot(ld, ld)
    p4 = bdot(p2, p2)
    p8 = bdot(p4, p4)
    lp = bdot(ld, p2)
    x = [a - b - c_ for a, b, c_ in zip(p2, ld, lp)]
    xp = bdot(x, p4)
    x = [a + b + c_ for a, b, c_ in zip(x, p4, xp)]
    xp = bdot(x, p8)
    t = [eye + a + b + c_ for a, b, c_ in zip(x, p8, xp)]
    c32 = [jnp.where(jnp.logical_and(same32, jnp.logical_not(same16)), m, 0.0) for m in lows]
    t = [a - b for a, b in zip(t, bdot(bdot(t, c32), t))]
    c64 = [jnp.where(same32, 0.0, m) for m in lows]
    t = [a - b for a, b in zip(t, bdot(bdot(t, c64), t))]
    return [a - eye for a in t]


def _gdn_local_kernel(q_ref, k_ref, v_ref, bd_ref, cwq_ref, cwk_ref, cwv_ref, alog_ref, dtb_ref,
                      u_s, w_s, at_s, qd_s, kd_s, egl_s, cq_s, ck_s, cv_s, qn_s, kn_s, vn_s, gc_s, bt_s, *, tt):
    nc = tt // LIN_CHUNK
    hd = GDN_HEAD_DIM
    h = pl.program_id(0)

    @pl.when(pl.program_id(1) == 0)
    def _():
        cq_s[...] = jnp.zeros_like(cq_s)
        ck_s[...] = jnp.zeros_like(ck_s)
        cv_s[...] = jnp.zeros_like(cv_s)

    def conv_silu(x_ref, cw_ref, carry, dst):
        x = x_ref[...]
        y, y_top = _causal_conv_rows(x, cw_ref[...], carry[...])
        carry[...] = x[tt - SUBLANES:]
        dst[...] = _silu(y)
        dst[0:SUBLANES, :] = _silu(y_top)

    conv_silu(q_ref, cwq_ref, cq_s, qn_s)
    conv_silu(k_ref, cwk_ref, ck_s, kn_s)
    conv_silu(v_ref, cwv_ref, cv_s, vn_s)
    q = qn_s[...]
    k = kn_s[...]
    qn_s[...] = q * lax.rsqrt(jnp.sum(q * q, -1, keepdims=True) + RMS_EPS) * (hd ** -0.5)
    kn_s[...] = k * lax.rsqrt(jnp.sum(k * k, -1, keepdims=True) + RMS_EPS)

    bd = bd_ref[...]
    lane = lax.broadcasted_iota(jnp.int32, bd.shape, 1)
    beta_in = jnp.sum(jnp.where(lane == h, bd, 0.0), -1, keepdims=True)
    dec_in = jnp.sum(jnp.where(lane == GDN_HEADS + h, bd, 0.0), -1, keepdims=True)
    a_log = alog_ref[...][:, 0:1]
    dt_bias = dtb_ref[...][:, 0:1]
    xs = dec_in + dt_bias
    softplus = jnp.maximum(xs, 0.0) + jnp.log(1.0 + jnp.exp(-jnp.abs(xs)))
    g = -jnp.exp(a_log) * softplus
    gc_s[...] = _chunk_cumsum(jnp.broadcast_to(g, (tt, LANES)), LIN_CHUNK)
    bt_s[...] = jnp.broadcast_to(_sigmoid(beta_in), (tt, LANES))

    ri = lax.broadcasted_iota(jnp.int32, (LIN_CHUNK, LIN_CHUNK), 0)
    ci = lax.broadcasted_iota(jnp.int32, (LIN_CHUNK, LIN_CHUNK), 1)
    causal = ri >= ci
    strict = ri > ci

    for g0 in range(0, nc, GDN_CHUNK_GROUP):
        rows = [slice(c * LIN_CHUNK, (c + 1) * LIN_CHUNK) for c in range(g0, g0 + GDN_CHUNK_GROUP)]
        kc = [kn_s[r, :] for r in rows]
        gc = [gc_s[r, :] for r in rows]
        beta = [bt_s[r, :] for r in rows]
        decay = []
        for a in gc:
            diff = a[:, 0:LIN_CHUNK] - a.T[0:LIN_CHUNK, :]
            decay.append(jnp.where(causal, jnp.exp(jnp.where(causal, diff, 0.0)), 0.0))
        kb = [a * b for a, b in zip(kc, beta)]
        kcb = [a.astype(BF16) for a in kc]
        kk = [_dot_nt(a.astype(BF16), b) for a, b in zip(kb, kcb)]
        qk = [_dot_nt(qn_s[r, :].astype(BF16), b) for r, b in zip(rows, kcb)]
        lows = [jnp.where(strict, a * d, 0.0) for a, d in zip(kk, decay)]
        for r, a, d in zip(rows, qk, decay):
            at_s[r, :] = (a * d).astype(BF16)
        xinv = [a.astype(BF16) for a in _unit_lower_inverse_minus_eye(lows)]
        egc = [jnp.exp(a) for a in gc]
        vb = [vn_s[r, :] * b for r, b in zip(rows, beta)]
        kbe = [a * e for a, e in zip(kb, egc)]
        xu = [_dot(x, a.astype(BF16)) for x, a in zip(xinv, vb)]
        xw = [_dot(x, a.astype(BF16)) for x, a in zip(xinv, kbe)]
        for i, r in enumerate(rows):
            c = g0 + i
            u_s[r, :] = vb[i] + xu[i]
            w_s[r, :] = (kbe[i] + xw[i]).astype(BF16)
            qd_s[r, :] = (qn_s[r, :] * egc[i]).astype(BF16)
            g_last = gc[i][LIN_CHUNK - 1:LIN_CHUNK, :]
            kd_s[r, :] = (kc[i] * jnp.exp(g_last - gc[i])).astype(BF16)
            egl_s[c * SUBLANES:(c + 1) * SUBLANES, :] = jnp.broadcast_to(jnp.exp(g_last), (SUBLANES, LANES))


def _gdn_scan_kernel(u_ref, w_ref, at_ref, qd_ref, kd_ref, egl_ref, z_ref, ng_ref, o_ref, st_ref, *, tt):
    nc = tt // LIN_CHUNK
    hd = GDN_HEAD_DIM

    @pl.when(pl.program_id(0) == 0)
    def _():
        st_ref[...] = jnp.zeros_like(st_ref)

    ng = ng_ref[...]

    def scan_body(c, carry):
        r0 = pl.multiple_of(c * LIN_CHUNK, LIN_CHUNK)
        rows = pl.ds(r0, LIN_CHUNK)
        e0 = pl.multiple_of(c * SUBLANES, SUBLANES)
        heads = range(GDN_HEADS)
        cols = [slice(h * hd, (h + 1) * hd) for h in heads]
        st = [st_ref[h] for h in heads]
        stb = [a.astype(BF16) for a in st]
        ws = [_dot(w_ref[rows, c_], b) for c_, b in zip(cols, stb)]
        qs = [_dot(qd_ref[rows, c_], b) for c_, b in zip(cols, stb)]
        vnb = [(u_ref[rows, c_] - a).astype(BF16) for c_, a in zip(cols, ws)]
        av = [_dot(at_ref[h, rows, :], b) for h, b in zip(heads, vnb)]
        kv = [_dot_tn(kd_ref[rows, c_], b) for c_, b in zip(cols, vnb)]
        for h in heads:
            st_ref[h] = st[h] * egl_ref[h, pl.ds(e0, 1), :] + kv[h]
            o = qs[h] + av[h]
            o = o * lax.rsqrt(jnp.mean(o * o, -1, keepdims=True) + RMS_EPS) * ng
            o_ref[rows, cols[h]] = (o * _silu(z_ref[rows, cols[h]])).astype(o_ref.dtype)
        return carry

    lax.fori_loop(0, nc, scan_body, 0)


def _gdn(proj, bd, conv_w, a_log, dt_bias, norm_g, tt_local, tt_scan):
    s = proj.shape[0]
    nh = GDN_HEADS
    hd = GDN_HEAD_DIM
    tt = tt_local
    per_head = lambda a: jnp.broadcast_to(a.astype(F32)[:, None, None], (nh, 1, LANES))
    blk = lambda off: pl.BlockSpec((tt, hd), lambda h, i: (i, off + h))
    cw = lambda off: pl.BlockSpec((GDN_CONV, hd), lambda h, i: (0, off + h))
    scal = pl.BlockSpec((None, 1, LANES), lambda h, i: (h, 0, 0))
    head_blk = pl.BlockSpec((tt, hd), lambda h, i: (i, h))
    u, w, at, qd, kd, egl = pl.pallas_call(
        functools.partial(_gdn_local_kernel, tt=tt),
        grid=(nh, s // tt),
        in_specs=[blk(0), blk(nh), blk(2 * nh),
                  pl.BlockSpec((tt, LANES), lambda h, i: (i, 0)),
                  cw(0), cw(nh), cw(2 * nh), scal, scal],
        out_specs=[head_blk, head_blk,
                   pl.BlockSpec((None, tt, LIN_CHUNK), lambda h, i: (h, i, 0)),
                   head_blk, head_blk,
                   pl.BlockSpec((None, tt // LIN_CHUNK * SUBLANES, LANES), lambda h, i: (h, i, 0))],
        out_shape=[jax.ShapeDtypeStruct((s, nh * hd), F32), jax.ShapeDtypeStruct((s, nh * hd), BF16),
                   jax.ShapeDtypeStruct((nh, s, LIN_CHUNK), BF16),
                   jax.ShapeDtypeStruct((s, nh * hd), BF16), jax.ShapeDtypeStruct((s, nh * hd), BF16),
                   jax.ShapeDtypeStruct((nh, s // LIN_CHUNK * SUBLANES, LANES), F32)],
        scratch_shapes=[pltpu.VMEM((SUBLANES, hd), F32), pltpu.VMEM((SUBLANES, hd), F32),
                        pltpu.VMEM((SUBLANES, hd), F32),
                        pltpu.VMEM((tt, hd), F32), pltpu.VMEM((tt, hd), F32), pltpu.VMEM((tt, hd), F32),
                        pltpu.VMEM((tt, LANES), F32), pltpu.VMEM((tt, LANES), F32)],
        compiler_params=_params(("parallel", "arbitrary")),
        name="gdn_local",
    )(proj, proj, proj, bd, conv_w, conv_w, conv_w, per_head(a_log), per_head(dt_bias))
    tt = tt_scan
    wide = pl.BlockSpec((tt, nh * hd), lambda i: (i, 0))
    return pl.pallas_call(
        functools.partial(_gdn_scan_kernel, tt=tt),
        grid=(s // tt,),
        in_specs=[wide, wide,
                  pl.BlockSpec((nh, tt, LIN_CHUNK), lambda i: (0, i, 0)),
                  wide, wide,
                  pl.BlockSpec((nh, tt // LIN_CHUNK * SUBLANES, LANES), lambda i: (0, i, 0)),
                  pl.BlockSpec((tt, nh * hd), lambda i: (i, 3)),
                  pl.BlockSpec((1, hd), lambda i: (0, 0))],
        out_specs=wide,
        out_shape=jax.ShapeDtypeStruct((s, nh * hd), BF16),
        scratch_shapes=[pltpu.VMEM((nh, hd, hd), F32)],
        compiler_params=_params(("arbitrary",)),
        name="gdn_scan",
    )(u, w, at, qd, kd, egl, proj, norm_g.reshape(1, hd))


S5_LANE_CHUNK = 1024
S5_GROUPS_PER_TILE = LANES // S5_GROUP_CH


def _s5_kernel(u_ref, bre_ref, bim_ref, cre_ref, cim_ref, are_ref, aim_ref, pre_ref, pim_ref, d_ref,
               o_ref, nat_s, up_s, xr_s, xi_s, cr_s, ci_s, gr_s, gi_s, *, tt):
    nseg = SUBLANES
    seg = tt // nseg
    nstate = S5_GROUPS * S5_STATE
    ntile = S5_CHANNELS // LANES
    spt = S5_GROUPS_PER_TILE * S5_STATE

    @pl.when(pl.program_id(0) == 0)
    def _():
        gr_s[...] = jnp.zeros_like(gr_s)
        gi_s[...] = jnp.zeros_like(gi_s)

    for t in range(ntile):
        nat_s[t] = u_ref[:, t * LANES:(t + 1) * LANES]
    for t in range(ntile):
        for j in range(seg):
            up_s[t, j * nseg:(j + 1) * nseg, :] = nat_s[t, pl.ds(j, nseg, stride=seg), :]
    for t in range(ntile):
        ub = up_s[t].astype(BF16)
        xr_s[:, t * spt:(t + 1) * spt] = _dot(ub, bre_ref[t])
        xi_s[:, t * spt:(t + 1) * spt] = _dot(ub, bim_ref[t])

    for lc in range(nstate // S5_LANE_CHUNK):
        cols = slice(lc * S5_LANE_CHUNK, (lc + 1) * S5_LANE_CHUNK)
        ar = are_ref[:, cols]
        ai = aim_ref[:, cols]

        def local_step(j, carry):
            xr, xi = carry
            r0 = pl.multiple_of(j * nseg, nseg)
            nr = ar * xr - ai * xi + xr_s[pl.ds(r0, nseg), cols]
            ni = ar * xi + ai * xr + xi_s[pl.ds(r0, nseg), cols]
            xr_s[pl.ds(r0, nseg), cols] = nr
            xi_s[pl.ds(r0, nseg), cols] = ni
            return nr, ni

        zero = jnp.zeros((nseg, S5_LANE_CHUNK), F32)
        er, ei = lax.fori_loop(0, seg, local_step, (zero, zero))
        tiles = range(lc * S5_LANE_CHUNK // LANES, (lc + 1) * S5_LANE_CHUNK // LANES)
        pr = jnp.concatenate([pre_ref[t, seg - 1:seg, :] for t in tiles], axis=1)
        pi = jnp.concatenate([pim_ref[t, seg - 1:seg, :] for t in tiles], axis=1)
        cr = gr_s[:, cols]
        ci = gi_s[:, cols]
        for r in range(nseg):
            cr_s[r:r + 1, cols] = cr
            ci_s[r:r + 1, cols] = ci
            cr, ci = (pr * cr - pi * ci + er[r:r + 1], pr * ci + pi * cr + ei[r:r + 1])
        gr_s[:, cols] = cr
        gi_s[:, cols] = ci
        c_re = cr_s[:, cols]
        c_im = ci_s[:, cols]

        def fix_step(j, carry):
            r0 = pl.multiple_of(j * nseg, nseg)
            pjr = jnp.concatenate([pre_ref[t, pl.ds(j, 1), :] for t in tiles], axis=1)
            pji = jnp.concatenate([pim_ref[t, pl.ds(j, 1), :] for t in tiles], axis=1)
            xr_s[pl.ds(r0, nseg), cols] = xr_s[pl.ds(r0, nseg), cols] + pjr * c_re - pji * c_im
            xi_s[pl.ds(r0, nseg), cols] = xi_s[pl.ds(r0, nseg), cols] + pjr * c_im + pji * c_re
            return carry

        lax.fori_loop(0, seg, fix_step, 0)

    for t in range(ntile):
        xr = xr_s[:, t * spt:(t + 1) * spt].astype(BF16)
        xi = xi_s[:, t * spt:(t + 1) * spt].astype(BF16)
        y = _dot(xr, cre_ref[t]) - _dot(xi, cim_ref[t]) + d_ref[:, t * LANES:(t + 1) * LANES] * up_s[t]
        up_s[t] = 0.5 * y * (1.0 + jnp.tanh(math.sqrt(2.0 / math.pi) * (y + 0.044715 * (y * y * y))))
    for t in range(ntile):
        for j in range(seg):
            nat_s[t, pl.ds(j, nseg, stride=seg), :] = up_s[t, j * nseg:(j + 1) * nseg, :]
    for t in range(ntile):
        o_ref[:, t * LANES:(t + 1) * LANES] = nat_s[t]


def _s5(proj, col0, a_re, a_im, b_re, b_im, c_re, c_im, d, log_step, tt):
    s = proj.shape[0]
    seg = tt // SUBLANES
    nstate = S5_GROUPS * S5_STATE
    ntile = S5_CHANNELS // LANES
    gpt = S5_GROUPS_PER_TILE
    a_re, a_im, b_re, b_im, c_re, c_im, d = (t.astype(F32) for t in (a_re, a_im, b_re, b_im, c_re, c_im, d))
    step = jnp.exp(log_step.astype(F32))[:, None]
    mag = jnp.exp(a_re * step)
    ab_re = mag * jnp.cos(a_im * step)
    ab_im = mag * jnp.sin(a_im * step)
    den = jnp.square(a_re) + jnp.square(a_im)
    z_re = ((ab_re - 1.0) * a_re + ab_im * a_im) / den
    z_im = (ab_im * a_re - (ab_re - 1.0) * a_im) / den
    bb_re = z_re[..., None] * b_re - z_im[..., None] * b_im
    bb_im = z_re[..., None] * b_im + z_im[..., None] * b_re

    def block_diag_in(bb):
        bt = bb.reshape(ntile, gpt, S5_STATE, S5_GROUP_CH)
        eye = jnp.eye(gpt, dtype=F32)
        m = jnp.einsum('tgpc,gh->tgchp', bt, eye)
        return m.reshape(ntile, gpt * S5_GROUP_CH, gpt * S5_STATE).astype(BF16)

    def block_diag_out(cc):
        ct = cc.reshape(ntile, gpt, S5_GROUP_CH, S5_STATE)
        eye = jnp.eye(gpt, dtype=F32)
        m = jnp.einsum('tgcp,gh->tgphc', ct, eye)
        return m.reshape(ntile, gpt * S5_STATE, gpt * S5_GROUP_CH).astype(BF16)

    def pow_step(carry, _):
        pr, pi = carry
        nr, ni = pr * ab_re - pi * ab_im, pr * ab_im + pi * ab_re
        return (nr, ni), (pr, pi)

    _, (pw_re, pw_im) = lax.scan(pow_step, (ab_re, ab_im), None, length=seg)
    pw_re = pw_re.reshape(seg, nstate // LANES, LANES).transpose(1, 0, 2)
    pw_im = pw_im.reshape(seg, nstate // LANES, LANES).transpose(1, 0, 2)
    full = lambda shape: pl.BlockSpec(shape, lambda i: (0,) * len(shape))
    return pl.pallas_call(
        functools.partial(_s5_kernel, tt=tt),
        grid=(s // tt,),
        in_specs=[pl.BlockSpec((tt, S5_CHANNELS), lambda i: (i, col0)),
                  full((ntile, LANES, gpt * S5_STATE)), full((ntile, LANES, gpt * S5_STATE)),
                  full((ntile, gpt * S5_STATE, LANES)), full((ntile, gpt * S5_STATE, LANES)),
                  full((1, nstate)), full((1, nstate)),
                  full((nstate // LANES, seg, LANES)), full((nstate // LANES, seg, LANES)),
                  full((1, S5_CHANNELS))],
        out_specs=pl.BlockSpec((tt, S5_CHANNELS), lambda i: (i, 0)),
        out_shape=jax.ShapeDtypeStruct((s, S5_CHANNELS), F32),
        scratch_shapes=[pltpu.VMEM((ntile, tt, LANES), F32), pltpu.VMEM((ntile, tt, LANES), F32),
                        pltpu.VMEM((tt, nstate), F32), pltpu.VMEM((tt, nstate), F32),
                        pltpu.VMEM((SUBLANES, nstate), F32), pltpu.VMEM((SUBLANES, nstate), F32),
                        pltpu.VMEM((1, nstate), F32), pltpu.VMEM((1, nstate), F32)],
        compiler_params=_params(("arbitrary",)),
        name="s5",
    )(proj, block_diag_in(bb_re), block_diag_in(bb_im), block_diag_out(c_re), block_diag_out(c_im),
      ab_re.reshape(1, nstate), ab_im.reshape(1, nstate), pw_re, pw_im, d.reshape(1, S5_CHANNELS))


def _glu_kernel(y_ref, w_ref, b_ref, o_ref):
    y = y_ref[...]
    t = _dot(y.astype(BF16), w_ref[...]) + b_ref[...]
    o_ref[...] = (y * _sigmoid(t)).astype(o_ref.dtype)


def _glu(y, w, b, tm):
    m, n = y.shape
    return pl.pallas_call(
        _glu_kernel,
        grid=(m // tm,),
        in_specs=[pl.BlockSpec((tm, n), lambda i: (i, 0)),
                  pl.BlockSpec((n, n), lambda i: (0, 0)),
                  pl.BlockSpec((1, n), lambda i: (0, 0))],
        out_specs=pl.BlockSpec((tm, n), lambda i: (i, 0)),
        out_shape=jax.ShapeDtypeStruct((m, n), BF16),
        compiler_params=_params(("parallel",)),
        name="glu",
    )(y, w, b.reshape(1, n))


def _pad_cols(w, n):
    return jnp.concatenate([w, jnp.zeros((w.shape[0], n - w.shape[1]), w.dtype)], axis=1)


def _split_cols(w, sizes):
    offs = [0]
    for sz in sizes:
        offs.append(offs[-1] + sz)
    return [w[:, offs[i]:offs[i + 1]] for i in range(len(sizes))]


def _even_mixer(xb, w_in, w_gate2, b_gate, norm_g):
    hk, hv, hm = GLA_HEADS * GLA_DK, GLA_HEADS * GLA_DV, MOBA_HEADS * MOBA_HEAD_DIM
    gq, gk, gv, g_lr, gr, mq, mk, mv = _split_cols(w_in, (hk, hk, hv, GLA_GATE_RANK, hv, hm, hm, hm))
    w_main = jnp.concatenate([gq, gk, gv, gr, mq, mk, mv], axis=1).astype(BF16)
    proj = _matmul(xb, w_main, F32, 1024, 1024)
    glr = _matmul(xb, _pad_cols(g_lr, LANES).astype(BF16), F32, 1024, LANES)
    gla_out = _gla(proj, glr, w_gate2, b_gate, norm_g, 512)
    moba_out = _moba(proj, (2 * hk + 2 * hv) // LANES)
    return jnp.concatenate([gla_out, moba_out], axis=-1)


def _odd_mixer(xb, w_in, conv_w, a_log, dt_bias, norm_g, a_re, a_im, b_re, b_im, c_re, c_im, d, log_step,
               glu_w, glu_b):
    hq = GDN_HEADS * GDN_HEAD_DIM
    qkv, z, beta, decay, u = _split_cols(w_in, (3 * hq, hq, GDN_HEADS, GDN_HEADS, S5_CHANNELS))
    w_main = jnp.concatenate([qkv, z, u], axis=1).astype(BF16)
    proj = _matmul(xb, w_main, F32, 1024, 1024)
    bd = _matmul(xb, _pad_cols(jnp.concatenate([beta, decay], axis=1), LANES).astype(BF16), F32, 1024, LANES)
    gdn_out = _gdn(proj, bd, conv_w, a_log, dt_bias, norm_g, 512, 512)
    y = _s5(proj, 4 * hq // S5_CHANNELS, a_re, a_im, b_re, b_im, c_re, c_im, d, log_step, 512)
    s5_out = _glu(y, glu_w.astype(BF16), glu_b, 512)
    return jnp.concatenate([gdn_out, s5_out], axis=-1)


def kernel(x, even_w_in, gla_w_gate2, gla_b_gate, gla_norm_g, even_w_out, odd_w_in, gdn_conv_w, gdn_a_log,
           gdn_dt_bias, gdn_norm_g, s5_a_re, s5_a_im, s5_b_re, s5_b_im, s5_c_re, s5_c_im, s5_d, s5_log_step,
           s5_glu_w, s5_glu_b, odd_w_out, ln_mix_g, ln_mix_b, ffn_w_up, ffn_conv_w, ffn_w_down, ln_ffn_g, ln_ffn_b):
    bsz, seq, dm = x.shape
    outs = []
    for b in range(bsz):
        xf = x[b]
        xb = xf.astype(BF16)
        for i in range(DEPTH):
            j = i // 2
            if i % 2 == 0:
                mix_in = _even_mixer(xb, even_w_in[j], gla_w_gate2[j], gla_b_gate[j], gla_norm_g[j])
                w_out = even_w_out[j]
            else:
                mix_in = _odd_mixer(xb, odd_w_in[j], gdn_conv_w[j], gdn_a_log[j], gdn_dt_bias[j], gdn_norm_g[j],
                                    s5_a_re[j], s5_a_im[j], s5_b_re[j], s5_b_im[j], s5_c_re[j], s5_c_im[j],
                                    s5_d[j], s5_log_step[j], s5_glu_w[j], s5_glu_b[j])
                w_out = odd_w_out[j]
            xf, xb = _matmul_res_ln(mix_in, w_out.astype(BF16), xf, ln_mix_g[i], ln_mix_b[i], 512, dm)
            mid = _ffn_up(xb, ffn_w_up, ffn_conv_w, i, 1024, 512)
            xf, xb = _matmul_res_ln(mid, ffn_w_down[i].astype(BF16), xf, ln_ffn_g[i], ln_ffn_b[i], 512, D_FF // 4)
        outs.append(xf)
    return jnp.stack(outs, axis=0)
```

```python
import functools
import math

import jax
import jax.numpy as jnp
from jax import lax
from jax.experimental import pallas as pl
from jax.experimental.pallas import tpu as pltpu

F32 = jnp.float32
BF16 = jnp.bfloat16
HIGHEST = lax.Precision.HIGHEST

DEPTH = 4
GLA_HEADS, GLA_DK, GLA_DV, GLA_GATE_RANK, GLA_GATE_NORM = 8, 64, 128, 16, 16.0
MOBA_HEADS, MOBA_HEAD_DIM, MOBA_BLOCK, MOBA_TOPK = 8, 128, 256, 3
GDN_HEADS, GDN_HEAD_DIM, GDN_CONV = 8, 128, 4
S5_CHANNELS, S5_GROUP_CH, S5_STATE = 1024, 16, 64
S5_GROUPS = S5_CHANNELS // S5_GROUP_CH
LIN_CHUNK = 64
D_FF = 5632
FFN_CONV = 3
DEEPNORM_ALPHA = (2 * DEPTH) ** 0.25
LN_EPS = 1e-5
RMS_EPS = 1e-6

LANES = 128
SUBLANES = 8
VMEM_LIMIT_BYTES = 56 * 1024 * 1024
NEG_BIG = -1e30


def _params(semantics):
    return pltpu.CompilerParams(dimension_semantics=semantics, vmem_limit_bytes=VMEM_LIMIT_BYTES)


def _dot(a, b, precision=None):
    return jnp.dot(a, b, preferred_element_type=F32, precision=precision)


def _dot_nt(a, b, precision=None):
    return lax.dot_general(a, b, (((1,), (1,)), ((), ())), preferred_element_type=F32, precision=precision)


def _dot_tn(a, b, precision=None):
    return lax.dot_general(a, b, (((0,), (0,)), ((), ())), preferred_element_type=F32, precision=precision)


def _sigmoid(x):
    return 1.0 / (1.0 + jnp.exp(-x))


def _silu(x):
    return x * _sigmoid(x)


def _chunk_cumsum(x, chunk):
    pos = lax.broadcasted_iota(jnp.int32, x.shape, 0) % chunk
    k = 1
    while k < chunk:
        x = x + jnp.where(pos >= k, pltpu.roll(x, k, 0), 0.0)
        k *= 2
    return x


def _mm_kernel(a_ref, b_ref, o_ref):
    o_ref[...] = _dot(a_ref[...].astype(BF16), b_ref[...]).astype(o_ref.dtype)


def _matmul(a, b, out_dtype, tm, tn):
    m, k = a.shape
    n = b.shape[1]
    return pl.pallas_call(
        _mm_kernel,
        grid=(m // tm, n // tn),
        in_specs=[pl.BlockSpec((tm, k), lambda i, j: (i, 0)),
                  pl.BlockSpec((k, tn), lambda i, j: (0, j))],
        out_specs=pl.BlockSpec((tm, tn), lambda i, j: (i, j)),
        out_shape=jax.ShapeDtypeStruct((m, n), out_dtype),
        compiler_params=_params(("parallel", "parallel")),
        name="matmul",
    )(a, b)


def _mm_ln_kernel(a_ref, b_ref, x_ref, g_ref, bt_ref, of_ref, ob_ref, acc_ref, *, nk):
    k = pl.program_id(1)

    @pl.when(k == 0)
    def _():
        acc_ref[...] = jnp.zeros_like(acc_ref)

    acc_ref[...] += _dot(a_ref[...], b_ref[...])

    @pl.when(k == nk - 1)
    def _():
        y = DEEPNORM_ALPHA * x_ref[...] + acc_ref[...]
        mu = jnp.mean(y, -1, keepdims=True)
        d = y - mu
        var = jnp.mean(d * d, -1, keepdims=True)
        o = d * lax.rsqrt(var + LN_EPS) * g_ref[...] + bt_ref[...]
        of_ref[...] = o
        ob_ref[...] = o.astype(BF16)


def _matmul_res_ln(a, b, x, g, bt, tm, tk):
    m, k = a.shape
    n = b.shape[1]
    nk = k // tk
    return pl.pallas_call(
        functools.partial(_mm_ln_kernel, nk=nk),
        grid=(m // tm, nk),
        in_specs=[pl.BlockSpec((tm, tk), lambda i, kk: (i, kk)),
                  pl.BlockSpec((tk, n), lambda i, kk: (kk, 0)),
                  pl.BlockSpec((tm, n), lambda i, kk: (i, 0)),
                  pl.BlockSpec((1, n), lambda i, kk: (0, 0)),
                  pl.BlockSpec((1, n), lambda i, kk: (0, 0))],
        out_specs=[pl.BlockSpec((tm, n), lambda i, kk: (i, 0)),
                   pl.BlockSpec((tm, n), lambda i, kk: (i, 0))],
        out_shape=[jax.ShapeDtypeStruct((m, n), F32), jax.ShapeDtypeStruct((m, n), BF16)],
        scratch_shapes=[pltpu.VMEM((tm, n), F32)],
        compiler_params=_params(("parallel", "arbitrary")),
        name="matmul_res_ln",
    )(a, b, x, g.reshape(1, n), bt.reshape(1, n))


def _mm2_ln_kernel(a1_ref, a2_ref, b_ref, x_ref, g_ref, bt_ref, of_ref, ob_ref):
    k1 = a1_ref.shape[1]
    acc = _dot(a1_ref[...], b_ref[0:k1, :]) + _dot(a2_ref[...], b_ref[k1:, :])
    y = DEEPNORM_ALPHA * x_ref[...] + acc
    mu = jnp.mean(y, -1, keepdims=True)
    d = y - mu
    var = jnp.mean(d * d, -1, keepdims=True)
    o = d * lax.rsqrt(var + LN_EPS) * g_ref[...] + bt_ref[...]
    of_ref[...] = o
    ob_ref[...] = o.astype(BF16)


def _matmul2_res_ln(a1, a2, b, x, g, bt, tm):
    m, k1 = a1.shape
    k2 = a2.shape[1]
    n = b.shape[1]
    row = lambda w: pl.BlockSpec((tm, w), lambda i: (i, 0))
    return pl.pallas_call(
        _mm2_ln_kernel,
        grid=(m // tm,),
        in_specs=[row(k1), row(k2),
                  pl.BlockSpec((k1 + k2, n), lambda i: (0, 0)),
                  row(n),
                  pl.BlockSpec((1, n), lambda i: (0, 0)),
                  pl.BlockSpec((1, n), lambda i: (0, 0))],
        out_specs=[row(n), row(n)],
        out_shape=[jax.ShapeDtypeStruct((m, n), F32), jax.ShapeDtypeStruct((m, n), BF16)],
        compiler_params=_params(("parallel",)),
        name="matmul2_res_ln",
    )(a1, a2, b, x, g.reshape(1, n), bt.reshape(1, n))


def _causal_conv_rows(h, w, prev):
    width = w.shape[0]
    out = w[width - 1:width] * h
    top = jnp.concatenate([prev, h[0:SUBLANES]], axis=0)
    out_top = w[width - 1:width] * top
    for s in range(1, width):
        wj = w[width - 1 - s:width - s]
        out = out + wj * pltpu.roll(h, s, 0)
        out_top = out_top + wj * pltpu.roll(top, s, 0)
    return out, out_top[SUBLANES:]


def _ffn_up_kernel(x_ref, wg_ref, wv_ref, cg_ref, cv_ref, o_ref, carry_g, carry_v, wgb_s, wvb_s):
    @pl.when(pl.program_id(1) == 0)
    def _():
        carry_g[...] = jnp.zeros_like(carry_g)
        carry_v[...] = jnp.zeros_like(carry_v)
        wgb_s[...] = wg_ref[...].astype(BF16)
        wvb_s[...] = wv_ref[...].astype(BF16)

    x = x_ref[...]
    tm = x.shape[0]
    hg = _dot(x, wgb_s[...])
    hv = _dot(x, wvb_s[...])
    g, g_top = _causal_conv_rows(hg, cg_ref[...], carry_g[...])
    v, v_top = _causal_conv_rows(hv, cv_ref[...], carry_v[...])
    carry_g[...] = hg[tm - SUBLANES:]
    carry_v[...] = hv[tm - SUBLANES:]
    o_ref[...] = (_silu(g) * v).astype(o_ref.dtype)
    o_ref[0:SUBLANES, :] = (_silu(g_top) * v_top).astype(o_ref.dtype)


def _ffn_up(xb, w_up, conv_w, layer, tm, tn):
    m, k = xb.shape
    d_ff = w_up.shape[2] // 2
    nt = d_ff // tn
    return pl.pallas_call(
        _ffn_up_kernel,
        grid=(nt, m // tm),
        in_specs=[pl.BlockSpec((tm, k), lambda j, i: (i, 0)),
                  pl.BlockSpec((None, k, tn), lambda j, i: (layer, 0, j)),
                  pl.BlockSpec((None, k, tn), lambda j, i: (layer, 0, j + nt)),
                  pl.BlockSpec((None, FFN_CONV, tn), lambda j, i: (layer, 0, j)),
                  pl.BlockSpec((None, FFN_CONV, tn), lambda j, i: (layer, 0, j + nt))],
        out_specs=pl.BlockSpec((tm, tn), lambda j, i: (i, j)),
        out_shape=jax.ShapeDtypeStruct((m, d_ff), BF16),
        scratch_shapes=[pltpu.VMEM((SUBLANES, tn), F32), pltpu.VMEM((SUBLANES, tn), F32),
                        pltpu.VMEM((k, tn), BF16), pltpu.VMEM((k, tn), BF16)],
        compiler_params=_params(("parallel", "arbitrary")),
        name="ffn_up",
    )(xb, w_up, w_up, conv_w, conv_w)


def _gla_kernel(q_ref, k_ref, v_ref, r_ref, glr_ref, wg2_ref, bg_ref, ng_ref, o_ref,
                st_ref, qe_s, ke_s, kd_s, vb_s, ecl_s, *, tt):
    nc = tt // LIN_CHUNK
    hk = GLA_HEADS * GLA_DK

    @pl.when(pl.program_id(0) == 0)
    def _():
        st_ref[...] = jnp.zeros_like(st_ref)

    z = _dot(glr_ref[...], wg2_ref[...], HIGHEST) + bg_ref[...]
    log_a = (jnp.minimum(z, 0.0) - jnp.log(1.0 + jnp.exp(-jnp.abs(z)))) / GLA_GATE_NORM
    cum = _chunk_cumsum(log_a, LIN_CHUNK)
    q = q_ref[...] * (GLA_DK ** -0.5)
    k = k_ref[...]
    qe_s[...] = (q * jnp.exp(cum)).astype(BF16)
    ke_s[...] = (k * jnp.exp(-cum)).astype(BF16)
    for c in range(nc):
        rows = slice(c * LIN_CHUNK, (c + 1) * LIN_CHUNK)
        cl = cum[(c + 1) * LIN_CHUNK - 1:(c + 1) * LIN_CHUNK, :]
        kd_s[rows, :] = (k[rows] * jnp.exp(cl - cum[rows])).astype(BF16)
        ecl_s[c * SUBLANES:(c + 1) * SUBLANES, :] = jnp.broadcast_to(jnp.exp(cl), (SUBLANES, hk))
    vb_s[...] = v_ref[...].astype(BF16)

    ri = lax.broadcasted_iota(jnp.int32, (LIN_CHUNK, LIN_CHUNK), 0)
    ci = lax.broadcasted_iota(jnp.int32, (LIN_CHUNK, LIN_CHUNK), 1)
    causal = ri >= ci
    lane = lax.broadcasted_iota(jnp.int32, (LIN_CHUNK, LANES), 1)
    lo_half = lane < GLA_DK
    ng = ng_ref[...]

    def chunk_body(c, carry):
        r0 = pl.multiple_of(c * LIN_CHUNK, LIN_CHUNK)
        e0 = pl.multiple_of(c * SUBLANES, SUBLANES)
        for h in range(GLA_HEADS):
            pcol = slice((h // 2) * LANES, (h // 2 + 1) * LANES)
            keep = lo_half if h % 2 == 0 else jnp.logical_not(lo_half)
            qm = jnp.where(keep, qe_s[pl.ds(r0, LIN_CHUNK), pcol], 0)
            kdm = jnp.where(keep, kd_s[pl.ds(r0, LIN_CHUNK), pcol], 0)
            ke = ke_s[pl.ds(r0, LIN_CHUNK), pcol]
            vh = vb_s[pl.ds(r0, LIN_CHUNK), h * GLA_DV:(h + 1) * GLA_DV]
            st = st_ref[h]
            attn = jnp.where(causal, _dot_nt(qm, ke), 0.0)
            o = _dot(attn.astype(BF16), vh) + _dot_nt(qm, st.astype(BF16))
            st_ref[h] = st * ecl_s[pl.ds(e0, 1), pcol] + _dot_tn(vh, kdm)
            o = o * lax.rsqrt(jnp.mean(o * o, -1, keepdims=True) + RMS_EPS) * ng
            gate = r_ref[pl.ds(r0, LIN_CHUNK), h * GLA_DV:(h + 1) * GLA_DV]
            o_ref[pl.ds(r0, LIN_CHUNK), h * GLA_DV:(h + 1) * GLA_DV] = (o * _silu(gate)).astype(o_ref.dtype)
        return carry

    lax.fori_loop(0, nc, chunk_body, 0)


def _gla(proj, glr, w_gate2, b_gate, norm_g, tt):
    s = proj.shape[0]
    hk = GLA_HEADS * GLA_DK
    hv = GLA_HEADS * GLA_DV
    wg2 = jnp.zeros((LANES, hk), F32).at[:GLA_GATE_RANK].set(w_gate2)
    return pl.pallas_call(
        functools.partial(_gla_kernel, tt=tt),
        grid=(s // tt,),
        in_specs=[pl.BlockSpec((tt, hk), lambda i: (i, 0)),
                  pl.BlockSpec((tt, hk), lambda i: (i, 1)),
                  pl.BlockSpec((tt, hv), lambda i: (i, 1)),
                  pl.BlockSpec((tt, hv), lambda i: (i, 2)),
                  pl.BlockSpec((tt, LANES), lambda i: (i, 0)),
                  pl.BlockSpec((LANES, hk), lambda i: (0, 0)),
                  pl.BlockSpec((1, hk), lambda i: (0, 0)),
                  pl.BlockSpec((1, GLA_DV), lambda i: (0, 0))],
        out_specs=pl.BlockSpec((tt, hv), lambda i: (i, 0)),
        out_shape=jax.ShapeDtypeStruct((s, hv), BF16),
        scratch_shapes=[pltpu.VMEM((GLA_HEADS, GLA_DV, LANES), F32),
                        pltpu.VMEM((tt, hk), BF16), pltpu.VMEM((tt, hk), BF16), pltpu.VMEM((tt, hk), BF16),
                        pltpu.VMEM((tt, hv), BF16),
                        pltpu.VMEM((tt // LIN_CHUNK * SUBLANES, hk), F32)],
        compiler_params=_params(("arbitrary",)),
        name="gla",
    )(proj, proj, proj, proj, glr, wg2, b_gate.reshape(1, hk), norm_g.reshape(1, GLA_DV))


def _split3(x):
    hi = x.astype(BF16).astype(F32)
    rest = x - hi
    mid = rest.astype(BF16).astype(F32)
    lo = (rest - mid).astype(BF16).astype(F32)
    return hi, mid, lo


MOBA_SEL_LANES = 32
MOBA_ROW_LANE = 96
MOBA_COL_LANE = 99
LOG2E = 1.4426950408889634
MOBA_GROUP = 8


def _moba_kernel(q_ref, k_ref, v_ref, slope_ref, o_ref, kmean_s, kx_s, vt_s, vtb_s, qxt_s, qxo_s, s_s, so_s, acc_s,
                 *, n_blk):
    blk = MOBA_BLOCK
    hd = MOBA_HEAD_DIM
    i = pl.program_id(1)
    sl2 = slope_ref[...][:, 0:1] * LOG2E
    s_hi, s_mid, s_lo = _split3(sl2)

    @pl.when(i == 0)
    def _():
        lane = lax.broadcasted_iota(jnp.int32, (blk, LANES), 1)
        key_f = lax.broadcasted_iota(jnp.int32, (blk, LANES), 0).astype(F32)
        k_extra = jnp.where(lane == MOBA_ROW_LANE, -s_hi, jnp.where(lane == MOBA_ROW_LANE + 1, -s_mid,
                  jnp.where(lane == MOBA_ROW_LANE + 2, -s_lo,
                  jnp.where(jnp.logical_and(lane >= MOBA_COL_LANE, lane < MOBA_COL_LANE + 3), key_f, 0.0))))
        kmean_s[...] = jnp.zeros_like(kmean_s)
        for n in range(n_blk):
            rows = slice(n * blk, (n + 1) * blk)
            kb = k_ref[rows, :]
            kmean_s[n:n + 1, :] = jnp.mean(kb, axis=0, keepdims=True)
            onehot = jnp.logical_or(lane == n, jnp.logical_or(lane == MOBA_SEL_LANES + n,
                                                               lane == 2 * MOBA_SEL_LANES + n))
            kx_s[rows, 0:hd] = kb.astype(BF16)
            kx_s[rows, hd:] = jnp.where(onehot, 1.0, k_extra).astype(BF16)
            gcols = slice((n % MOBA_GROUP) * blk, (n % MOBA_GROUP + 1) * blk)
            vt = v_ref[rows, :].T.astype(BF16)
            vt_s[n // MOBA_GROUP, :, gcols] = vt
            vtb_s[n] = vt

    q = q_ref[...]
    nb = lax.broadcasted_iota(jnp.int32, (MOBA_SEL_LANES, blk), 0)
    nb_f = nb.astype(F32)
    gate = _dot_nt(kmean_s[...], q, HIGHEST)
    gate = jnp.where(nb < i, gate, -jnp.inf)
    chosen = jnp.zeros((MOBA_SEL_LANES, blk), jnp.bool_)
    for _ in range(MOBA_TOPK):
        best = jnp.max(gate, axis=0, keepdims=True)
        idx = jnp.min(jnp.where(gate == best, nb_f, float(MOBA_SEL_LANES)), axis=0, keepdims=True)
        hit = nb_f == idx
        chosen = jnp.logical_or(chosen, hit)
        gate = jnp.where(hit, -jnp.inf, gate)
    sel = jnp.where(jnp.logical_and(chosen, nb < i), -sl2 * ((i - nb) * blk).astype(F32), NEG_BIG)
    b_hi, b_mid, b_lo = _split3(sel)
    qry_f = lax.broadcasted_iota(jnp.int32, (MOBA_SEL_LANES, blk), 1).astype(F32)
    tail = jnp.where(nb < 3, qry_f, jnp.where(nb == 3, s_hi, jnp.where(nb == 4, s_mid,
           jnp.where(nb == 5, s_lo, 0.0))))
    qt = (q * (hd ** -0.5 * LOG2E)).T.astype(BF16)
    qxt_s[0:hd, :] = qt
    qxt_s[hd:, :] = jnp.concatenate([b_hi, b_mid, b_lo, tail], axis=0).astype(BF16)
    qxo_s[0:hd, :] = qt
    qxo_s[hd:, :] = jnp.concatenate([jnp.zeros((3 * MOBA_SEL_LANES, blk), F32), tail], axis=0).astype(BF16)

    def fold(x):
        return x.reshape(x.shape[0] // SUBLANES, SUBLANES, blk)

    key = lax.broadcasted_iota(jnp.int32, (blk, blk), 0)
    qry = lax.broadcasted_iota(jnp.int32, (blk, blk), 1)
    own0 = pl.multiple_of(i * blk, blk)
    s_own = jnp.where(key <= qry, _dot(kx_s[pl.ds(own0, blk), :], qxo_s[...]), NEG_BIG)
    so_s[...] = s_own

    ngrp = (i + MOBA_GROUP - 1) // MOBA_GROUP
    gkeys = MOBA_GROUP * blk

    def score_group(g, mx):
        k0 = pl.multiple_of(g * gkeys, gkeys)
        s = _dot(kx_s[pl.ds(k0, gkeys), :], qxt_s[...])
        s_s[pl.ds(g * MOBA_GROUP, MOBA_GROUP)] = s.reshape(MOBA_GROUP, blk, blk)
        return jnp.maximum(mx, jnp.max(fold(s), axis=0))

    mx = lax.fori_loop(0, ngrp, score_group, jnp.max(fold(s_own), axis=0))
    m_row = jnp.max(mx, axis=0, keepdims=True)

    p_own = jnp.exp2(so_s[...] - m_row)
    acc_s[...] = _dot(vtb_s[i], p_own.astype(BF16))

    def prob_group(g, ls):
        s = s_s[pl.ds(g * MOBA_GROUP, MOBA_GROUP)].reshape(gkeys, blk)
        p = jnp.exp2(s - m_row)
        acc_s[...] += _dot(vt_s[g], p.astype(BF16))
        return ls + jnp.sum(fold(p), axis=0)

    ls = lax.fori_loop(0, ngrp, prob_group, jnp.sum(fold(p_own), axis=0))
    l_row = jnp.sum(ls, axis=0, keepdims=True)
    o_ref[...] = (acc_s[...] / l_row).T.astype(o_ref.dtype)


def _moba(proj, col0):
    s = proj.shape[0]
    n_blk = s // MOBA_BLOCK
    assert n_blk <= MOBA_SEL_LANES
    slopes = jnp.exp2(-8.0 * jnp.arange(1, MOBA_HEADS + 1, dtype=F32) / MOBA_HEADS)
    slopes = jnp.broadcast_to(slopes[:, None, None], (MOBA_HEADS, 1, LANES))
    return pl.pallas_call(
        functools.partial(_moba_kernel, n_blk=n_blk),
        grid=(MOBA_HEADS, n_blk),
        in_specs=[pl.BlockSpec((MOBA_BLOCK, MOBA_HEAD_DIM), lambda h, i: (i, col0 + h)),
                  pl.BlockSpec((s, MOBA_HEAD_DIM), lambda h, i: (0, col0 + MOBA_HEADS + h)),
                  pl.BlockSpec((s, MOBA_HEAD_DIM), lambda h, i: (0, col0 + 2 * MOBA_HEADS + h)),
                  pl.BlockSpec((None, 1, LANES), lambda h, i: (h, 0, 0))],
        out_specs=pl.BlockSpec((MOBA_BLOCK, MOBA_HEAD_DIM), lambda h, i: (i, h)),
        out_shape=jax.ShapeDtypeStruct((s, MOBA_HEADS * MOBA_HEAD_DIM), BF16),
        scratch_shapes=[pltpu.VMEM((MOBA_SEL_LANES, MOBA_HEAD_DIM), F32),
                        pltpu.VMEM((s, MOBA_HEAD_DIM + LANES), BF16),
                        pltpu.VMEM((n_blk // MOBA_GROUP, MOBA_HEAD_DIM, MOBA_GROUP * MOBA_BLOCK), BF16),
                        pltpu.VMEM((n_blk, MOBA_HEAD_DIM, MOBA_BLOCK), BF16),
                        pltpu.VMEM((MOBA_HEAD_DIM + LANES, MOBA_BLOCK), BF16),
                        pltpu.VMEM((MOBA_HEAD_DIM + LANES, MOBA_BLOCK), BF16),
                        pltpu.VMEM((n_blk, MOBA_BLOCK, MOBA_BLOCK), F32),
                        pltpu.VMEM((MOBA_BLOCK, MOBA_BLOCK), F32),
                        pltpu.VMEM((MOBA_HEAD_DIM, MOBA_BLOCK), F32)],
        compiler_params=_params(("parallel", "arbitrary")),
        name="moba",
    )(proj, proj, proj, slopes)


GDN_CHUNK_GROUP = 8


def _unit_lower_inverse_minus_eye(lows):
    n = lows[0].shape[0]
    r = lax.broadcasted_iota(jnp.int32, (n, n), 0)
    c = lax.broadcasted_iota(jnp.int32, (n, n), 1)
    eye = jnp.where(r == c, 1.0, 0.0)
    same16 = (r // 16) == (c // 16)
    same32 = (r // 32) == (c // 32)
    bdot = lambda a, b: [_dot(x.astype(BF16), y.astype(BF16)) for x, y in zip(a, b)]
    ld = [jnp.where(same16, m, 0.0) for m in lows]
    p2 = bdot(ld, ld)
    p4 = bdot(p2, p2)
    p8 = bdot(p4, p4)
    lp = bdot(ld, p2)
    x = [a - b - c_ for a, b, c_ in zip(p2, ld, lp)]
    xp = bdot(x, p4)
    x = [a + b + c_ for a, b, c_ in zip(x, p4, xp)]
    xp = bdot(x, p8)
    t = [eye + a + b + c_ for a, b, c_ in zip(x, p8, xp)]
    c32 = [jnp.where(jnp.logical_and(same32, jnp.logical_not(same16)), m, 0.0) for m in lows]
    t = [a - b for a, b in zip(t, bdot(bdot(t, c32), t))]
    c64 = [jnp.where(same32, 0.0, m) for m in lows]
    t = [a - b for a, b in zip(t, bdot(bdot(t, c64), t))]
    return [a - eye for a in t]


def _gdn_local_kernel(q_ref, k_ref, v_ref, bd_ref, cwq_ref, cwk_ref, cwv_ref, alog_ref, dtb_ref,
                      u_s, w_s, at_s, qd_s, kd_s, egl_s, cq_s, ck_s, cv_s, qn_s, kn_s, vn_s, gc_s, bt_s, *, tt):
    nc = tt // LIN_CHUNK
    hd = GDN_HEAD_DIM
    h = pl.program_id(0)

    @pl.when(pl.program_id(1) == 0)
    def _():
        cq_s[...] = jnp.zeros_like(cq_s)
        ck_s[...] = jnp.zeros_like(ck_s)
        cv_s[...] = jnp.zeros_like(cv_s)

    def conv_silu(x_ref, cw_ref, carry, dst):
        x = x_ref[...]
        y, y_top = _causal_conv_rows(x, cw_ref[...], carry[...])
        carry[...] = x[tt - SUBLANES:]
        dst[...] = _silu(y)
        dst[0:SUBLANES, :] = _silu(y_top)

    conv_silu(q_ref, cwq_ref, cq_s, qn_s)
    conv_silu(k_ref, cwk_ref, ck_s, kn_s)
    conv_silu(v_ref, cwv_ref, cv_s, vn_s)
    q = qn_s[...]
    k = kn_s[...]
    qn_s[...] = q * lax.rsqrt(jnp.sum(q * q, -1, keepdims=True) + RMS_EPS) * (hd ** -0.5)
    kn_s[...] = k * lax.rsqrt(jnp.sum(k * k, -1, keepdims=True) + RMS_EPS)

    bd = bd_ref[...]
    lane = lax.broadcasted_iota(jnp.int32, bd.shape, 1)
    beta_in = jnp.sum(jnp.where(lane == h, bd, 0.0), -1, keepdims=True)
    dec_in = jnp.sum(jnp.where(lane == GDN_HEADS + h, bd, 0.0), -1, keepdims=True)
    a_log = alog_ref[...][:, 0:1]
    dt_bias = dtb_ref[...][:, 0:1]
    xs = dec_in + dt_bias
    softplus = jnp.maximum(xs, 0.0) + jnp.log(1.0 + jnp.exp(-jnp.abs(xs)))
    g = -jnp.exp(a_log) * softplus
    gc_s[...] = _chunk_cumsum(jnp.broadcast_to(g, (tt, LANES)), LIN_CHUNK)
    bt_s[...] = jnp.broadcast_to(_sigmoid(beta_in), (tt, LANES))

    ri = lax.broadcasted_iota(jnp.int32, (LIN_CHUNK, LIN_CHUNK), 0)
    ci = lax.broadcasted_iota(jnp.int32, (LIN_CHUNK, LIN_CHUNK), 1)
    causal = ri >= ci
    strict = ri > ci

    for g0 in range(0, nc, GDN_CHUNK_GROUP):
        rows = [slice(c * LIN_CHUNK, (c + 1) * LIN_CHUNK) for c in range(g0, g0 + GDN_CHUNK_GROUP)]
        kc = [kn_s[r, :] for r in rows]
        gc = [gc_s[r, :] for r in rows]
        beta = [bt_s[r, :] for r in rows]
        decay = []
        for a in gc:
            diff = a[:, 0:LIN_CHUNK] - a.T[0:LIN_CHUNK, :]
            decay.append(jnp.where(causal, jnp.exp(jnp.where(causal, diff, 0.0)), 0.0))
        kb = [a * b for a, b in zip(kc, beta)]
        kcb = [a.astype(BF16) for a in kc]
        kk = [_dot_nt(a.astype(BF16), b) for a, b in zip(kb, kcb)]
        qk = [_dot_nt(qn_s[r, :].astype(BF16), b) for r, b in zip(rows, kcb)]
        lows = [jnp.where(strict, a * d, 0.0) for a, d in zip(kk, decay)]
        for r, a, d in zip(rows, qk, decay):
            at_s[r, :] = (a * d).astype(BF16)
        xinv = [a.astype(BF16) for a in _unit_lower_inverse_minus_eye(lows)]
        egc = [jnp.exp(a) for a in gc]
        vb = [vn_s[r, :] * b for r, b in zip(rows, beta)]
        kbe = [a * e for a, e in zip(kb, egc)]
        xu = [_dot(x, a.astype(BF16)) for x, a in zip(xinv, vb)]
        xw = [_dot(x, a.astype(BF16)) for x, a in zip(xinv, kbe)]
        for i, r in enumerate(rows):
            c = g0 + i
            u_s[r, :] = vb[i] + xu[i]
            w_s[r, :] = (kbe[i] + xw[i]).astype(BF16)
            qd_s[r, :] = (qn_s[r, :] * egc[i]).astype(BF16)
            g_last = gc[i][LIN_CHUNK - 1:LIN_CHUNK, :]
            kd_s[r, :] = (kc[i] * jnp.exp(g_last - gc[i])).astype(BF16)
            egl_s[c * SUBLANES:(c + 1) * SUBLANES, :] = jnp.broadcast_to(jnp.exp(g_last), (SUBLANES, LANES))


def _gdn_scan_kernel(u_ref, w_ref, at_ref, qd_ref, kd_ref, egl_ref, z_ref, ng_ref, o_ref, st_ref, *, tt):
    nc = tt // LIN_CHUNK
    hd = GDN_HEAD_DIM

    @pl.when(pl.program_id(0) == 0)
    def _():
        st_ref[...] = jnp.zeros_like(st_ref)

    ng = ng_ref[...]

    def scan_body(c, carry):
        r0 = pl.multiple_of(c * LIN_CHUNK, LIN_CHUNK)
        rows = pl.ds(r0, LIN_CHUNK)
        e0 = pl.multiple_of(c * SUBLANES, SUBLANES)
        heads = range(GDN_HEADS)
        cols = [slice(h * hd, (h + 1) * hd) for h in heads]
        st = [st_ref[h] for h in heads]
        stb = [a.astype(BF16) for a in st]
        ws = [_dot(w_ref[rows, c_], b) for c_, b in zip(cols, stb)]
        qs = [_dot(qd_ref[rows, c_], b) for c_, b in zip(cols, stb)]
        vnb = [(u_ref[rows, c_] - a).astype(BF16) for c_, a in zip(cols, ws)]
        av = [_dot(at_ref[h, rows, :], b) for h, b in zip(heads, vnb)]
        kv = [_dot_tn(kd_ref[rows, c_], b) for c_, b in zip(cols, vnb)]
        for h in heads:
            st_ref[h] = st[h] * egl_ref[h, pl.ds(e0, 1), :] + kv[h]
            o = qs[h] + av[h]
            o = o * lax.rsqrt(jnp.mean(o * o, -1, keepdims=True) + RMS_EPS) * ng
            o_ref[rows, cols[h]] = (o * _silu(z_ref[rows, cols[h]])).astype(o_ref.dtype)
        return carry

    lax.fori_loop(0, nc, scan_body, 0)


def _gdn(proj, bd, conv_w, a_log, dt_bias, norm_g, tt_local, tt_scan):
    s = proj.shape[0]
    nh = GDN_HEADS
    hd = GDN_HEAD_DIM
    tt = tt_local
    per_head = lambda a: jnp.broadcast_to(a.astype(F32)[:, None, None], (nh, 1, LANES))
    blk = lambda off: pl.BlockSpec((tt, hd), lambda h, i: (i, off + h))
    cw = lambda off: pl.BlockSpec((GDN_CONV, hd), lambda h, i: (0, off + h))
    scal = pl.BlockSpec((None, 1, LANES), lambda h, i: (h, 0, 0))
    head_blk = pl.BlockSpec((tt, hd), lambda h, i: (i, h))
    u, w, at, qd, kd, egl = pl.pallas_call(
        functools.partial(_gdn_local_kernel, tt=tt),
        grid=(nh, s // tt),
        in_specs=[blk(0), blk(nh), blk(2 * nh),
                  pl.BlockSpec((tt, LANES), lambda h, i: (i, 0)),
                  cw(0), cw(nh), cw(2 * nh), scal, scal],
        out_specs=[head_blk, head_blk,
                   pl.BlockSpec((None, tt, LIN_CHUNK), lambda h, i: (h, i, 0)),
                   head_blk, head_blk,
                   pl.BlockSpec((None, tt // LIN_CHUNK * SUBLANES, LANES), lambda h, i: (h, i, 0))],
        out_shape=[jax.ShapeDtypeStruct((s, nh * hd), F32), jax.ShapeDtypeStruct((s, nh * hd), BF16),
                   jax.ShapeDtypeStruct((nh, s, LIN_CHUNK), BF16),
                   jax.ShapeDtypeStruct((s, nh * hd), BF16), jax.ShapeDtypeStruct((s, nh * hd), BF16),
                   jax.ShapeDtypeStruct((nh, s // LIN_CHUNK * SUBLANES, LANES), F32)],
        scratch_shapes=[pltpu.VMEM((SUBLANES, hd), F32), pltpu.VMEM((SUBLANES, hd), F32),
                        pltpu.VMEM((SUBLANES, hd), F32),
                        pltpu.VMEM((tt, hd), F32), pltpu.VMEM((tt, hd), F32), pltpu.VMEM((tt, hd), F32),
                        pltpu.VMEM((tt, LANES), F32), pltpu.VMEM((tt, LANES), F32)],
        compiler_params=_params(("parallel", "arbitrary")),
        name="gdn_local",
    )(proj, proj, proj, bd, conv_w, conv_w, conv_w, per_head(a_log), per_head(dt_bias))
    tt = tt_scan
    wide = pl.BlockSpec((tt, nh * hd), lambda i: (i, 0))
    return pl.pallas_call(
        functools.partial(_gdn_scan_kernel, tt=tt),
        grid=(s // tt,),
        in_specs=[wide, wide,
                  pl.BlockSpec((nh, tt, LIN_CHUNK), lambda i: (0, i, 0)),
                  wide, wide,
                  pl.BlockSpec((nh, tt // LIN_CHUNK * SUBLANES, LANES), lambda i: (0, i, 0)),
                  pl.BlockSpec((tt, nh * hd), lambda i: (i, 3)),
                  pl.BlockSpec((1, hd), lambda i: (0, 0))],
        out_specs=wide,
        out_shape=jax.ShapeDtypeStruct((s, nh * hd), BF16),
        scratch_shapes=[pltpu.VMEM((nh, hd, hd), F32)],
        compiler_params=_params(("arbitrary",)),
        name="gdn_scan",
    )(u, w, at, qd, kd, egl, proj, norm_g.reshape(1, hd))


S5_LANE_CHUNK = 1024
S5_GROUPS_PER_TILE = LANES // S5_GROUP_CH


def _s5_kernel(u_ref, bre_ref, bim_ref, cre_ref, cim_ref, are_ref, aim_ref, pre_ref, pim_ref, d_ref,
               o_ref, nat_s, up_s, xr_s, xi_s, cr_s, ci_s, gr_s, gi_s, *, tt):
    nseg = SUBLANES
    seg = tt // nseg
    nstate = S5_GROUPS * S5_STATE
    ntile = S5_CHANNELS // LANES
    spt = S5_GROUPS_PER_TILE * S5_STATE

    @pl.when(pl.program_id(0) == 0)
    def _():
        gr_s[...] = jnp.zeros_like(gr_s)
        gi_s[...] = jnp.zeros_like(gi_s)

    for t in range(ntile):
        nat_s[t] = u_ref[:, t * LANES:(t + 1) * LANES]
    for t in range(ntile):
        for j in range(seg):
            up_s[t, j * nseg:(j + 1) * nseg, :] = nat_s[t, pl.ds(j, nseg, stride=seg), :]
    for t in range(ntile):
        ub = up_s[t].astype(BF16)
        xr_s[:, t * spt:(t + 1) * spt] = _dot(ub, bre_ref[t])
        xi_s[:, t * spt:(t + 1) * spt] = _dot(ub, bim_ref[t])

    for lc in range(nstate // S5_LANE_CHUNK):
        cols = slice(lc * S5_LANE_CHUNK, (lc + 1) * S5_LANE_CHUNK)
        ar = are_ref[:, cols]
        ai = aim_ref[:, cols]

        def local_step(j, carry):
            xr, xi = carry
            r0 = pl.multiple_of(j * nseg, nseg)
            nr = ar * xr - ai * xi + xr_s[pl.ds(r0, nseg), cols]
            ni = ar * xi + ai * xr + xi_s[pl.ds(r0, nseg), cols]
            xr_s[pl.ds(r0, nseg), cols] = nr
            xi_s[pl.ds(r0, nseg), cols] = ni
            return nr, ni

        zero = jnp.zeros((nseg, S5_LANE_CHUNK), F32)
        er, ei = lax.fori_loop(0, seg, local_step, (zero, zero))
        tiles = range(lc * S5_LANE_CHUNK // LANES, (lc + 1) * S5_LANE_CHUNK // LANES)
        pr = jnp.concatenate([pre_ref[t, seg - 1:seg, :] for t in tiles], axis=1)
        pi = jnp.concatenate([pim_ref[t, seg - 1:seg, :] for t in tiles], axis=1)
        cr = gr_s[:, cols]
        ci = gi_s[:, cols]
        for r in range(nseg):
            cr_s[r:r + 1, cols] = cr
            ci_s[r:r + 1, cols] = ci
            cr, ci = (pr * cr - pi * ci + er[r:r + 1], pr * ci + pi * cr + ei[r:r + 1])
        gr_s[:, cols] = cr
        gi_s[:, cols] = ci
        c_re = cr_s[:, cols]
        c_im = ci_s[:, cols]

        def fix_step(j, carry):
            r0 = pl.multiple_of(j * nseg, nseg)
            pjr = jnp.concatenate([pre_ref[t, pl.ds(j, 1), :] for t in tiles], axis=1)
            pji = jnp.concatenate([pim_ref[t, pl.ds(j, 1), :] for t in tiles], axis=1)
            xr_s[pl.ds(r0, nseg), cols] = xr_s[pl.ds(r0, nseg), cols] + pjr * c_re - pji * c_im
            xi_s[pl.ds(r0, nseg), cols] = xi_s[pl.ds(r0, nseg), cols] + pjr * c_im + pji * c_re
            return carry

        lax.fori_loop(0, seg, fix_step, 0)

    for t in range(ntile):
        xr = xr_s[:, t * spt:(t + 1) * spt].astype(BF16)
        xi = xi_s[:, t * spt:(t + 1) * spt].astype(BF16)
        y = _dot(xr, cre_ref[t]) - _dot(xi, cim_ref[t]) + d_ref[:, t * LANES:(t + 1) * LANES] * up_s[t]
        up_s[t] = 0.5 * y * (1.0 + jnp.tanh(math.sqrt(2.0 / math.pi) * (y + 0.044715 * (y * y * y))))
    for t in range(ntile):
        for j in range(seg):
            nat_s[t, pl.ds(j, nseg, stride=seg), :] = up_s[t, j * nseg:(j + 1) * nseg, :]
    for t in range(ntile):
        o_ref[:, t * LANES:(t + 1) * LANES] = nat_s[t]


def _s5(proj, col0, a_re, a_im, b_re, b_im, c_re, c_im, d, log_step, tt):
    s = proj.shape[0]
    seg = tt // SUBLANES
    nstate = S5_GROUPS * S5_STATE
    ntile = S5_CHANNELS // LANES
    gpt = S5_GROUPS_PER_TILE
    a_re, a_im, b_re, b_im, c_re, c_im, d = (t.astype(F32) for t in (a_re, a_im, b_re, b_im, c_re, c_im, d))
    step = jnp.exp(log_step.astype(F32))[:, None]
    mag = jnp.exp(a_re * step)
    ab_re = mag * jnp.cos(a_im * step)
    ab_im = mag * jnp.sin(a_im * step)
    den = jnp.square(a_re) + jnp.square(a_im)
    z_re = ((ab_re - 1.0) * a_re + ab_im * a_im) / den
    z_im = (ab_im * a_re - (ab_re - 1.0) * a_im) / den
    bb_re = z_re[..., None] * b_re - z_im[..., None] * b_im
    bb_im = z_re[..., None] * b_im + z_im[..., None] * b_re

    def block_diag_in(bb):
        bt = bb.reshape(ntile, gpt, S5_STATE, S5_GROUP_CH)
        eye = jnp.eye(gpt, dtype=F32)
        m = jnp.einsum('tgpc,gh->tgchp', bt, eye)
        return m.reshape(ntile, gpt * S5_GROUP_CH, gpt * S5_STATE).astype(BF16)

    def block_diag_out(cc):
        ct = cc.reshape(ntile, gpt, S5_GROUP_CH, S5_STATE)
        eye = jnp.eye(gpt, dtype=F32)
        m = jnp.einsum('tgcp,gh->tgphc', ct, eye)
        return m.reshape(ntile, gpt * S5_STATE, gpt * S5_GROUP_CH).astype(BF16)

    def pow_step(carry, _):
        pr, pi = carry
        nr, ni = pr * ab_re - pi * ab_im, pr * ab_im + pi * ab_re
        return (nr, ni), (pr, pi)

    _, (pw_re, pw_im) = lax.scan(pow_step, (ab_re, ab_im), None, length=seg)
    pw_re = pw_re.reshape(seg, nstate // LANES, LANES).transpose(1, 0, 2)
    pw_im = pw_im.reshape(seg, nstate // LANES, LANES).transpose(1, 0, 2)
    full = lambda shape: pl.BlockSpec(shape, lambda i: (0,) * len(shape))
    return pl.pallas_call(
        functools.partial(_s5_kernel, tt=tt),
        grid=(s // tt,),
        in_specs=[pl.BlockSpec((tt, S5_CHANNELS), lambda i: (i, col0)),
                  full((ntile, LANES, gpt * S5_STATE)), full((ntile, LANES, gpt * S5_STATE)),
                  full((ntile, gpt * S5_STATE, LANES)), full((ntile, gpt * S5_STATE, LANES)),
                  full((1, nstate)), full((1, nstate)),
                  full((nstate // LANES, seg, LANES)), full((nstate // LANES, seg, LANES)),
                  full((1, S5_CHANNELS))],
        out_specs=pl.BlockSpec((tt, S5_CHANNELS), lambda i: (i, 0)),
        out_shape=jax.ShapeDtypeStruct((s, S5_CHANNELS), F32),
        scratch_shapes=[pltpu.VMEM((ntile, tt, LANES), F32), pltpu.VMEM((ntile, tt, LANES), F32),
                        pltpu.VMEM((tt, nstate), F32), pltpu.VMEM((tt, nstate), F32),
                        pltpu.VMEM((SUBLANES, nstate), F32), pltpu.VMEM((SUBLANES, nstate), F32),
                        pltpu.VMEM((1, nstate), F32), pltpu.VMEM((1, nstate), F32)],
        compiler_params=_params(("arbitrary",)),
        name="s5",
    )(proj, block_diag_in(bb_re), block_diag_in(bb_im), block_diag_out(c_re), block_diag_out(c_im),
      ab_re.reshape(1, nstate), ab_im.reshape(1, nstate), pw_re, pw_im, d.reshape(1, S5_CHANNELS))


def _glu_kernel(y_ref, w_ref, b_ref, o_ref):
    y = y_ref[...]
    t = _dot(y.astype(BF16), w_ref[...]) + b_ref[...]
    o_ref[...] = (y * _sigmoid(t)).astype(o_ref.dtype)


def _glu(y, w, b, tm):
    m, n = y.shape
    return pl.pallas_call(
        _glu_kernel,
        grid=(m // tm,),
        in_specs=[pl.BlockSpec((tm, n), lambda i: (i, 0)),
                  pl.BlockSpec((n, n), lambda i: (0, 0)),
                  pl.BlockSpec((1, n), lambda i: (0, 0))],
        out_specs=pl.BlockSpec((tm, n), lambda i: (i, 0)),
        out_shape=jax.ShapeDtypeStruct((m, n), BF16),
        compiler_params=_params(("parallel",)),
        name="glu",
    )(y, w, b.reshape(1, n))


def _pad_cols(w, n):
    return jnp.concatenate([w, jnp.zeros((w.shape[0], n - w.shape[1]), w.dtype)], axis=1)


def _split_cols(w, sizes):
    offs = [0]
    for sz in sizes:
        offs.append(offs[-1] + sz)
    return [w[:, offs[i]:offs[i + 1]] for i in range(len(sizes))]


def _even_mixer(xb, w_in, w_gate2, b_gate, norm_g):
    hk, hv, hm = GLA_HEADS * GLA_DK, GLA_HEADS * GLA_DV, MOBA_HEADS * MOBA_HEAD_DIM
    gq, gk, gv, g_lr, gr, mq, mk, mv = _split_cols(w_in, (hk, hk, hv, GLA_GATE_RANK, hv, hm, hm, hm))
    w_main = jnp.concatenate([gq, gk, gv, gr, mq, mk, mv], axis=1).astype(BF16)
    proj = _matmul(xb, w_main, F32, 1024, 1024)
    glr = _matmul(xb, _pad_cols(g_lr, LANES).astype(BF16), F32, 1024, LANES)
    gla_out = _gla(proj, glr, w_gate2, b_gate, norm_g, 512)
    moba_out = _moba(proj, (2 * hk + 2 * hv) // LANES)
    return gla_out, moba_out


def _odd_mixer(xb, w_in, conv_w, a_log, dt_bias, norm_g, a_re, a_im, b_re, b_im, c_re, c_im, d, log_step,
               glu_w, glu_b):
    hq = GDN_HEADS * GDN_HEAD_DIM
    qkv, z, beta, decay, u = _split_cols(w_in, (3 * hq, hq, GDN_HEADS, GDN_HEADS, S5_CHANNELS))
    w_main = jnp.concatenate([qkv, z, u], axis=1).astype(BF16)
    proj = _matmul(xb, w_main, F32, 1024, 1024)
    bd = _matmul(xb, _pad_cols(jnp.concatenate([beta, decay], axis=1), LANES).astype(BF16), F32, 1024, LANES)
    gdn_out = _gdn(proj, bd, conv_w, a_log, dt_bias, norm_g, 512, 512)
    y = _s5(proj, 4 * hq // S5_CHANNELS, a_re, a_im, b_re, b_im, c_re, c_im, d, log_step, 512)
    s5_out = _glu(y, glu_w.astype(BF16), glu_b, 512)
    return gdn_out, s5_out


def kernel(x, even_w_in, gla_w_gate2, gla_b_gate, gla_norm_g, even_w_out, odd_w_in, gdn_conv_w, gdn_a_log,
           gdn_dt_bias, gdn_norm_g, s5_a_re, s5_a_im, s5_b_re, s5_b_im, s5_c_re, s5_c_im, s5_d, s5_log_step,
           s5_glu_w, s5_glu_b, odd_w_out, ln_mix_g, ln_mix_b, ffn_w_up, ffn_conv_w, ffn_w_down, ln_ffn_g, ln_ffn_b):
    bsz, seq, dm = x.shape
    outs = []
    for b in range(bsz):
        xf = x[b]
        xb = xf.astype(BF16)
        for i in range(DEPTH):
            j = i // 2
            if i % 2 == 0:
                mix_in = _even_mixer(xb, even_w_in[j], gla_w_gate2[j], gla_b_gate[j], gla_norm_g[j])
                w_out = even_w_out[j]
            else:
                mix_in = _odd_mixer(xb, odd_w_in[j], gdn_conv_w[j], gdn_a_log[j], gdn_dt_bias[j], gdn_norm_g[j],
                                    s5_a_re[j], s5_a_im[j], s5_b_re[j], s5_b_im[j], s5_c_re[j], s5_c_im[j],
                                    s5_d[j], s5_log_step[j], s5_glu_w[j], s5_glu_b[j])
                w_out = odd_w_out[j]
            xf, xb = _matmul2_res_ln(mix_in[0], mix_in[1], w_out.astype(BF16), xf, ln_mix_g[i], ln_mix_b[i], 512)
            mid = _ffn_up(xb, ffn_w_up, ffn_conv_w, i, 1024, 512)
            xf, xb = _matmul_res_ln(mid, ffn_w_down[i].astype(BF16), xf, ln_ffn_g[i], ln_ffn_b[i], 512, D_FF // 4)
        outs.append(xf)
    return jnp.stack(outs, axis=0)
```

```python
import functools
import math

import jax
import jax.numpy as jnp
from jax import lax
from jax.experimental import pallas as pl
from jax.experimental.pallas import tpu as pltpu

F32 = jnp.float32
BF16 = jnp.bfloat16
HIGHEST = lax.Precision.HIGHEST

DEPTH = 4
GLA_HEADS, GLA_DK, GLA_DV, GLA_GATE_RANK, GLA_GATE_NORM = 8, 64, 128, 16, 16.0
MOBA_HEADS, MOBA_HEAD_DIM, MOBA_BLOCK, MOBA_TOPK = 8, 128, 256, 3
GDN_HEADS, GDN_HEAD_DIM, GDN_CONV = 8, 128, 4
S5_CHANNELS, S5_GROUP_CH, S5_STATE = 1024, 16, 64
S5_GROUPS = S5_CHANNELS // S5_GROUP_CH
LIN_CHUNK = 64
D_FF = 5632
FFN_CONV = 3
DEEPNORM_ALPHA = (2 * DEPTH) ** 0.25
LN_EPS = 1e-5
RMS_EPS = 1e-6

LANES = 128
SUBLANES = 8
VMEM_LIMIT_BYTES = 56 * 1024 * 1024
NEG_BIG = -1e30


def _params(semantics):
    return pltpu.CompilerParams(dimension_semantics=semantics, vmem_limit_bytes=VMEM_LIMIT_BYTES)


def _dot(a, b, precision=None):
    return jnp.dot(a, b, preferred_element_type=F32, precision=precision)


def _dot_nt(a, b, precision=None):
    return lax.dot_general(a, b, (((1,), (1,)), ((), ())), preferred_element_type=F32, precision=precision)


def _dot_tn(a, b, precision=None):
    return lax.dot_general(a, b, (((0,), (0,)), ((), ())), preferred_element_type=F32, precision=precision)


def _sigmoid(x):
    return 1.0 / (1.0 + jnp.exp(-x))


def _silu(x):
    return x * _sigmoid(x)


def _chunk_cumsum(x, chunk):
    pos = lax.broadcasted_iota(jnp.int32, x.shape, 0) % chunk
    k = 1
    while k < chunk:
        x = x + jnp.where(pos >= k, pltpu.roll(x, k, 0), 0.0)
        k *= 2
    return x


def _mm_kernel(a_ref, b_ref, o_ref):
    o_ref[...] = _dot(a_ref[...].astype(BF16), b_ref[...]).astype(o_ref.dtype)


def _matmul(a, b, out_dtype, tm, tn):
    m, k = a.shape
    n = b.shape[1]
    return pl.pallas_call(
        _mm_kernel,
        grid=(m // tm, n // tn),
        in_specs=[pl.BlockSpec((tm, k), lambda i, j: (i, 0)),
                  pl.BlockSpec((k, tn), lambda i, j: (0, j))],
        out_specs=pl.BlockSpec((tm, tn), lambda i, j: (i, j)),
        out_shape=jax.ShapeDtypeStruct((m, n), out_dtype),
        compiler_params=_params(("parallel", "parallel")),
        name="matmul",
    )(a, b)


def _mm_ln_kernel(a_ref, b_ref, x_ref, g_ref, bt_ref, of_ref, ob_ref, acc_ref, *, nk):
    k = pl.program_id(1)

    @pl.when(k == 0)
    def _():
        acc_ref[...] = jnp.zeros_like(acc_ref)

    acc_ref[...] += _dot(a_ref[...], b_ref[...])

    @pl.when(k == nk - 1)
    def _():
        y = DEEPNORM_ALPHA * x_ref[...] + acc_ref[...]
        mu = jnp.mean(y, -1, keepdims=True)
        d = y - mu
        var = jnp.mean(d * d, -1, keepdims=True)
        o = d * lax.rsqrt(var + LN_EPS) * g_ref[...] + bt_ref[...]
        of_ref[...] = o
        ob_ref[...] = o.astype(BF16)


def _matmul_res_ln(a, b, x, g, bt, tm, tk):
    m, k = a.shape
    n = b.shape[1]
    nk = k // tk
    return pl.pallas_call(
        functools.partial(_mm_ln_kernel, nk=nk),
        grid=(m // tm, nk),
        in_specs=[pl.BlockSpec((tm, tk), lambda i, kk: (i, kk)),
                  pl.BlockSpec((tk, n), lambda i, kk: (kk, 0)),
                  pl.BlockSpec((tm, n), lambda i, kk: (i, 0)),
                  pl.BlockSpec((1, n), lambda i, kk: (0, 0)),
                  pl.BlockSpec((1, n), lambda i, kk: (0, 0))],
        out_specs=[pl.BlockSpec((tm, n), lambda i, kk: (i, 0)),
                   pl.BlockSpec((tm, n), lambda i, kk: (i, 0))],
        out_shape=[jax.ShapeDtypeStruct((m, n), F32), jax.ShapeDtypeStruct((m, n), BF16)],
        scratch_shapes=[pltpu.VMEM((tm, n), F32)],
        compiler_params=_params(("parallel", "arbitrary")),
        name="matmul_res_ln",
    )(a, b, x, g.reshape(1, n), bt.reshape(1, n))


def _mm2_ln_kernel(a1_ref, a2_ref, b_ref, x_ref, g_ref, bt_ref, of_ref, ob_ref):
    k1 = a1_ref.shape[1]
    acc = _dot(a1_ref[...], b_ref[0:k1, :]) + _dot(a2_ref[...], b_ref[k1:, :])
    y = DEEPNORM_ALPHA * x_ref[...] + acc
    mu = jnp.mean(y, -1, keepdims=True)
    d = y - mu
    var = jnp.mean(d * d, -1, keepdims=True)
    o = d * lax.rsqrt(var + LN_EPS) * g_ref[...] + bt_ref[...]
    of_ref[...] = o
    ob_ref[...] = o.astype(BF16)


def _matmul2_res_ln(a1, a2, b, x, g, bt, tm):
    m, k1 = a1.shape
    k2 = a2.shape[1]
    n = b.shape[1]
    row = lambda w: pl.BlockSpec((tm, w), lambda i: (i, 0))
    return pl.pallas_call(
        _mm2_ln_kernel,
        grid=(m // tm,),
        in_specs=[row(k1), row(k2),
                  pl.BlockSpec((k1 + k2, n), lambda i: (0, 0)),
                  row(n),
                  pl.BlockSpec((1, n), lambda i: (0, 0)),
                  pl.BlockSpec((1, n), lambda i: (0, 0))],
        out_specs=[row(n), row(n)],
        out_shape=[jax.ShapeDtypeStruct((m, n), F32), jax.ShapeDtypeStruct((m, n), BF16)],
        compiler_params=_params(("parallel",)),
        name="matmul2_res_ln",
    )(a1, a2, b, x, g.reshape(1, n), bt.reshape(1, n))


def _causal_conv_rows(h, w, prev):
    width = w.shape[0]
    out = w[width - 1:width] * h
    top = jnp.concatenate([prev, h[0:SUBLANES]], axis=0)
    out_top = w[width - 1:width] * top
    for s in range(1, width):
        wj = w[width - 1 - s:width - s]
        out = out + wj * pltpu.roll(h, s, 0)
        out_top = out_top + wj * pltpu.roll(top, s, 0)
    return out, out_top[SUBLANES:]


def _ffn_up_kernel(x_ref, wg_ref, wv_ref, cg_ref, cv_ref, o_ref, carry_g, carry_v, wgb_s, wvb_s):
    @pl.when(pl.program_id(1) == 0)
    def _():
        carry_g[...] = jnp.zeros_like(carry_g)
        carry_v[...] = jnp.zeros_like(carry_v)
        wgb_s[...] = wg_ref[...].astype(BF16)
        wvb_s[...] = wv_ref[...].astype(BF16)

    x = x_ref[...]
    tm = x.shape[0]
    hg = _dot(x, wgb_s[...])
    hv = _dot(x, wvb_s[...])
    g, g_top = _causal_conv_rows(hg, cg_ref[...], carry_g[...])
    v, v_top = _causal_conv_rows(hv, cv_ref[...], carry_v[...])
    carry_g[...] = hg[tm - SUBLANES:]
    carry_v[...] = hv[tm - SUBLANES:]
    o_ref[...] = (_silu(g) * v).astype(o_ref.dtype)
    o_ref[0:SUBLANES, :] = (_silu(g_top) * v_top).astype(o_ref.dtype)


def _ffn_up(xb, w_up, conv_w, layer, tm, tn):
    m, k = xb.shape
    d_ff = w_up.shape[2] // 2
    nt = d_ff // tn
    return pl.pallas_call(
        _ffn_up_kernel,
        grid=(nt, m // tm),
        in_specs=[pl.BlockSpec((tm, k), lambda j, i: (i, 0)),
                  pl.BlockSpec((None, k, tn), lambda j, i: (layer, 0, j)),
                  pl.BlockSpec((None, k, tn), lambda j, i: (layer, 0, j + nt)),
                  pl.BlockSpec((None, FFN_CONV, tn), lambda j, i: (layer, 0, j)),
                  pl.BlockSpec((None, FFN_CONV, tn), lambda j, i: (layer, 0, j + nt))],
        out_specs=pl.BlockSpec((tm, tn), lambda j, i: (i, j)),
        out_shape=jax.ShapeDtypeStruct((m, d_ff), BF16),
        scratch_shapes=[pltpu.VMEM((SUBLANES, tn), F32), pltpu.VMEM((SUBLANES, tn), F32),
                        pltpu.VMEM((k, tn), BF16), pltpu.VMEM((k, tn), BF16)],
        compiler_params=_params(("parallel", "arbitrary")),
        name="ffn_up",
    )(xb, w_up, w_up, conv_w, conv_w)


def _gla_kernel(q_ref, k_ref, v_ref, r_ref, glr_ref, wg2_ref, bg_ref, ng_ref, o_ref,
                st_ref, qe_s, ke_s, kd_s, vb_s, ecl_s, *, tt):
    nc = tt // LIN_CHUNK
    hk = GLA_HEADS * GLA_DK

    @pl.when(pl.program_id(0) == 0)
    def _():
        st_ref[...] = jnp.zeros_like(st_ref)

    z = _dot(glr_ref[...], wg2_ref[...], HIGHEST) + bg_ref[...]
    log_a = (jnp.minimum(z, 0.0) - jnp.log(1.0 + jnp.exp(-jnp.abs(z)))) / GLA_GATE_NORM
    cum = _chunk_cumsum(log_a, LIN_CHUNK)
    q = q_ref[...] * (GLA_DK ** -0.5)
    k = k_ref[...]
    qe_s[...] = (q * jnp.exp(cum)).astype(BF16)
    ke_s[...] = (k * jnp.exp(-cum)).astype(BF16)
    for c in range(nc):
        rows = slice(c * LIN_CHUNK, (c + 1) * LIN_CHUNK)
        cl = cum[(c + 1) * LIN_CHUNK - 1:(c + 1) * LIN_CHUNK, :]
        kd_s[rows, :] = (k[rows] * jnp.exp(cl - cum[rows])).astype(BF16)
        ecl_s[c * SUBLANES:(c + 1) * SUBLANES, :] = jnp.broadcast_to(jnp.exp(cl), (SUBLANES, hk))
    vb_s[...] = v_ref[...].astype(BF16)

    ri = lax.broadcasted_iota(jnp.int32, (LIN_CHUNK, LIN_CHUNK), 0)
    ci = lax.broadcasted_iota(jnp.int32, (LIN_CHUNK, LIN_CHUNK), 1)
    causal = ri >= ci
    lane = lax.broadcasted_iota(jnp.int32, (LIN_CHUNK, LANES), 1)
    lo_half = lane < GLA_DK
    ng = ng_ref[...]

    def chunk_body(c, carry):
        r0 = pl.multiple_of(c * LIN_CHUNK, LIN_CHUNK)
        e0 = pl.multiple_of(c * SUBLANES, SUBLANES)
        for h in range(GLA_HEADS):
            pcol = slice((h // 2) * LANES, (h // 2 + 1) * LANES)
            keep = lo_half if h % 2 == 0 else jnp.logical_not(lo_half)
            qm = jnp.where(keep, qe_s[pl.ds(r0, LIN_CHUNK), pcol], 0)
            kdm = jnp.where(keep, kd_s[pl.ds(r0, LIN_CHUNK), pcol], 0)
            ke = ke_s[pl.ds(r0, LIN_CHUNK), pcol]
            vh = vb_s[pl.ds(r0, LIN_CHUNK), h * GLA_DV:(h + 1) * GLA_DV]
            st = st_ref[h]
            attn = jnp.where(causal, _dot_nt(qm, ke), 0.0)
            o = _dot(attn.astype(BF16), vh) + _dot_nt(qm, st.astype(BF16))
            st_ref[h] = st * ecl_s[pl.ds(e0, 1), pcol] + _dot_tn(vh, kdm)
            o = o * lax.rsqrt(jnp.mean(o * o, -1, keepdims=True) + RMS_EPS) * ng
            gate = r_ref[pl.ds(r0, LIN_CHUNK), h * GLA_DV:(h + 1) * GLA_DV]
            o_ref[pl.ds(r0, LIN_CHUNK), h * GLA_DV:(h + 1) * GLA_DV] = (o * _silu(gate)).astype(o_ref.dtype)
        return carry

    lax.fori_loop(0, nc, chunk_body, 0)


def _gla(proj, glr, w_gate2, b_gate, norm_g, tt):
    s = proj.shape[0]
    hk = GLA_HEADS * GLA_DK
    hv = GLA_HEADS * GLA_DV
    wg2 = jnp.zeros((LANES, hk), F32).at[:GLA_GATE_RANK].set(w_gate2)
    return pl.pallas_call(
        functools.partial(_gla_kernel, tt=tt),
        grid=(s // tt,),
        in_specs=[pl.BlockSpec((tt, hk), lambda i: (i, 0)),
                  pl.BlockSpec((tt, hk), lambda i: (i, 1)),
                  pl.BlockSpec((tt, hv), lambda i: (i, 1)),
                  pl.BlockSpec((tt, hv), lambda i: (i, 2)),
                  pl.BlockSpec((tt, LANES), lambda i: (i, 0)),
                  pl.BlockSpec((LANES, hk), lambda i: (0, 0)),
                  pl.BlockSpec((1, hk), lambda i: (0, 0)),
                  pl.BlockSpec((1, GLA_DV), lambda i: (0, 0))],
        out_specs=pl.BlockSpec((tt, hv), lambda i: (i, 0)),
        out_shape=jax.ShapeDtypeStruct((s, hv), BF16),
        scratch_shapes=[pltpu.VMEM((GLA_HEADS, GLA_DV, LANES), F32),
                        pltpu.VMEM((tt, hk), BF16), pltpu.VMEM((tt, hk), BF16), pltpu.VMEM((tt, hk), BF16),
                        pltpu.VMEM((tt, hv), BF16),
                        pltpu.VMEM((tt // LIN_CHUNK * SUBLANES, hk), F32)],
        compiler_params=_params(("arbitrary",)),
        name="gla",
    )(proj, proj, proj, proj, glr, wg2, b_gate.reshape(1, hk), norm_g.reshape(1, GLA_DV))


def _split3(x):
    hi = x.astype(BF16).astype(F32)
    rest = x - hi
    mid = rest.astype(BF16).astype(F32)
    lo = (rest - mid).astype(BF16).astype(F32)
    return hi, mid, lo


MOBA_SEL_LANES = 32
MOBA_ROW_LANE = 96
MOBA_COL_LANE = 99
LOG2E = 1.4426950408889634
MOBA_GROUP = 8


def _moba_kernel(q_ref, k_ref, v_ref, slope_ref, o_ref, kmean_s, kx_s, vt_s, vtb_s, qxt_s, qxo_s, s_s, so_s, acc_s,
                 *, n_blk):
    blk = MOBA_BLOCK
    hd = MOBA_HEAD_DIM
    i = pl.program_id(1)
    sl2 = slope_ref[...][:, 0:1] * LOG2E
    s_hi, s_mid, s_lo = _split3(sl2)

    @pl.when(i == 0)
    def _():
        lane = lax.broadcasted_iota(jnp.int32, (blk, LANES), 1)
        key_f = lax.broadcasted_iota(jnp.int32, (blk, LANES), 0).astype(F32)
        k_extra = jnp.where(lane == MOBA_ROW_LANE, -s_hi, jnp.where(lane == MOBA_ROW_LANE + 1, -s_mid,
                  jnp.where(lane == MOBA_ROW_LANE + 2, -s_lo,
                  jnp.where(jnp.logical_and(lane >= MOBA_COL_LANE, lane < MOBA_COL_LANE + 3), key_f, 0.0))))
        kmean_s[...] = jnp.zeros_like(kmean_s)
        for n in range(n_blk):
            rows = slice(n * blk, (n + 1) * blk)
            kb = k_ref[rows, :]
            kmean_s[n:n + 1, :] = jnp.mean(kb, axis=0, keepdims=True)
            onehot = jnp.logical_or(lane == n, jnp.logical_or(lane == MOBA_SEL_LANES + n,
                                                               lane == 2 * MOBA_SEL_LANES + n))
            kx_s[rows, 0:hd] = kb.astype(BF16)
            kx_s[rows, hd:] = jnp.where(onehot, 1.0, k_extra).astype(BF16)
            gcols = slice((n % MOBA_GROUP) * blk, (n % MOBA_GROUP + 1) * blk)
            vt = v_ref[rows, :].T.astype(BF16)
            vt_s[n // MOBA_GROUP, :, gcols] = vt
            vtb_s[n] = vt

    q = q_ref[...]
    nb = lax.broadcasted_iota(jnp.int32, (MOBA_SEL_LANES, blk), 0)
    nb_f = nb.astype(F32)
    gate = _dot_nt(kmean_s[...], q, HIGHEST)
    gate = jnp.where(nb < i, gate, -jnp.inf)
    chosen = jnp.zeros((MOBA_SEL_LANES, blk), jnp.bool_)
    for _ in range(MOBA_TOPK):
        best = jnp.max(gate, axis=0, keepdims=True)
        idx = jnp.min(jnp.where(gate == best, nb_f, float(MOBA_SEL_LANES)), axis=0, keepdims=True)
        hit = nb_f == idx
        chosen = jnp.logical_or(chosen, hit)
        gate = jnp.where(hit, -jnp.inf, gate)
    sel = jnp.where(jnp.logical_and(chosen, nb < i), -sl2 * ((i - nb) * blk).astype(F32), NEG_BIG)
    b_hi, b_mid, b_lo = _split3(sel)
    qry_f = lax.broadcasted_iota(jnp.int32, (MOBA_SEL_LANES, blk), 1).astype(F32)
    tail = jnp.where(nb < 3, qry_f, jnp.where(nb == 3, s_hi, jnp.where(nb == 4, s_mid,
           jnp.where(nb == 5, s_lo, 0.0))))
    qt = (q * (hd ** -0.5 * LOG2E)).T.astype(BF16)
    qxt_s[0:hd, :] = qt
    qxt_s[hd:, :] = jnp.concatenate([b_hi, b_mid, b_lo, tail], axis=0).astype(BF16)
    qxo_s[0:hd, :] = qt
    qxo_s[hd:, :] = jnp.concatenate([jnp.zeros((3 * MOBA_SEL_LANES, blk), F32), tail], axis=0).astype(BF16)

    def fold(x):
        return x.reshape(x.shape[0] // SUBLANES, SUBLANES, blk)

    key = lax.broadcasted_iota(jnp.int32, (blk, blk), 0)
    qry = lax.broadcasted_iota(jnp.int32, (blk, blk), 1)
    own0 = pl.multiple_of(i * blk, blk)
    s_own = jnp.where(key <= qry, _dot(kx_s[pl.ds(own0, blk), :], qxo_s[...]), NEG_BIG)
    so_s[...] = s_own

    ngrp = (i + MOBA_GROUP - 1) // MOBA_GROUP
    gkeys = MOBA_GROUP * blk

    def score_group(g, mx):
        k0 = pl.multiple_of(g * gkeys, gkeys)
        s = _dot(kx_s[pl.ds(k0, gkeys), :], qxt_s[...])
        s_s[pl.ds(g * MOBA_GROUP, MOBA_GROUP)] = s.reshape(MOBA_GROUP, blk, blk)
        return jnp.maximum(mx, jnp.max(fold(s), axis=0))

    mx = lax.fori_loop(0, ngrp, score_group, jnp.max(fold(s_own), axis=0))
    m_row = jnp.max(mx, axis=0, keepdims=True)

    p_own = jnp.exp2(so_s[...] - m_row)
    acc_s[...] = _dot(vtb_s[i], p_own.astype(BF16))

    def prob_group(g, ls):
        s = s_s[pl.ds(g * MOBA_GROUP, MOBA_GROUP)].reshape(gkeys, blk)
        p = jnp.exp2(s - m_row)
        acc_s[...] += _dot(vt_s[g], p.astype(BF16))
        return ls + jnp.sum(fold(p), axis=0)

    ls = lax.fori_loop(0, ngrp, prob_group, jnp.sum(fold(p_own), axis=0))
    l_row = jnp.sum(ls, axis=0, keepdims=True)
    o_ref[...] = (acc_s[...] / l_row).T.astype(o_ref.dtype)


def _moba(proj, col0):
    s = proj.shape[0]
    n_blk = s // MOBA_BLOCK
    assert n_blk <= MOBA_SEL_LANES
    slopes = jnp.exp2(-8.0 * jnp.arange(1, MOBA_HEADS + 1, dtype=F32) / MOBA_HEADS)
    slopes = jnp.broadcast_to(slopes[:, None, None], (MOBA_HEADS, 1, LANES))
    return pl.pallas_call(
        functools.partial(_moba_kernel, n_blk=n_blk),
        grid=(MOBA_HEADS, n_blk),
        in_specs=[pl.BlockSpec((MOBA_BLOCK, MOBA_HEAD_DIM), lambda h, i: (i, col0 + h)),
                  pl.BlockSpec((s, MOBA_HEAD_DIM), lambda h, i: (0, col0 + MOBA_HEADS + h)),
                  pl.BlockSpec((s, MOBA_HEAD_DIM), lambda h, i: (0, col0 + 2 * MOBA_HEADS + h)),
                  pl.BlockSpec((None, 1, LANES), lambda h, i: (h, 0, 0))],
        out_specs=pl.BlockSpec((MOBA_BLOCK, MOBA_HEAD_DIM), lambda h, i: (i, h)),
        out_shape=jax.ShapeDtypeStruct((s, MOBA_HEADS * MOBA_HEAD_DIM), BF16),
        scratch_shapes=[pltpu.VMEM((MOBA_SEL_LANES, MOBA_HEAD_DIM), F32),
                        pltpu.VMEM((s, MOBA_HEAD_DIM + LANES), BF16),
                        pltpu.VMEM((n_blk // MOBA_GROUP, MOBA_HEAD_DIM, MOBA_GROUP * MOBA_BLOCK), BF16),
                        pltpu.VMEM((n_blk, MOBA_HEAD_DIM, MOBA_BLOCK), BF16),
                        pltpu.VMEM((MOBA_HEAD_DIM + LANES, MOBA_BLOCK), BF16),
                        pltpu.VMEM((MOBA_HEAD_DIM + LANES, MOBA_BLOCK), BF16),
                        pltpu.VMEM((n_blk, MOBA_BLOCK, MOBA_BLOCK), F32),
                        pltpu.VMEM((MOBA_BLOCK, MOBA_BLOCK), F32),
                        pltpu.VMEM((MOBA_HEAD_DIM, MOBA_BLOCK), F32)],
        compiler_params=_params(("parallel", "arbitrary")),
        name="moba",
    )(proj, proj, proj, slopes)


GDN_CHUNK_GROUP = 8


def _unit_lower_inverse_minus_eye(lows):
    n = lows[0].shape[0]
    r = lax.broadcasted_iota(jnp.int32, (n, n), 0)
    c = lax.broadcasted_iota(jnp.int32, (n, n), 1)
    eye = jnp.where(r == c, 1.0, 0.0)
    same16 = (r // 16) == (c // 16)
    same32 = (r // 32) == (c // 32)
    bdot = lambda a, b: [_dot(x.astype(BF16), y.astype(BF16)) for x, y in zip(a, b)]
    ld = [jnp.where(same16, m, 0.0) for m in lows]
    p2 = bdot(ld, ld)
    p4 = bdot(p2, p2)
    p8 = bdot(p4, p4)
    lp = bdot(ld, p2)
    x = [a - b - c_ for a, b, c_ in zip(p2, ld, lp)]
    xp = bdot(x, p4)
    x = [a + b + c_ for a, b, c_ in zip(x, p4, xp)]
    xp = bdot(x, p8)
    t = [eye + a + b + c_ for a, b, c_ in zip(x, p8, xp)]
    c32 = [jnp.where(jnp.logical_and(same32, jnp.logical_not(same16)), m, 0.0) for m in lows]
    t = [a - b for a, b in zip(t, bdot(bdot(t, c32), t))]
    c64 = [jnp.where(same32, 0.0, m) for m in lows]
    t = [a - b for a, b in zip(t, bdot(bdot(t, c64), t))]
    return [a - eye for a in t]


def _gdn_local_kernel(q_ref, k_ref, v_ref, bd_ref, cwq_ref, cwk_ref, cwv_ref, alog_ref, dtb_ref,
                      u_s, w_s, at_s, qd_s, kd_s, egl_s, cq_s, ck_s, cv_s, qn_s, kn_s, vn_s, gc_s, bt_s, *, tt):
    nc = tt // LIN_CHUNK
    hd = GDN_HEAD_DIM
    h = pl.program_id(0)

    @pl.when(pl.program_id(1) == 0)
    def _():
        cq_s[...] = jnp.zeros_like(cq_s)
        ck_s[...] = jnp.zeros_like(ck_s)
        cv_s[...] = jnp.zeros_like(cv_s)

    def conv_silu(x_ref, cw_ref, carry, dst):
        x = x_ref[...]
        y, y_top = _causal_conv_rows(x, cw_ref[...], carry[...])
        carry[...] = x[tt - SUBLANES:]
        dst[...] = _silu(y)
        dst[0:SUBLANES, :] = _silu(y_top)

    conv_silu(q_ref, cwq_ref, cq_s, qn_s)
    conv_silu(k_ref, cwk_ref, ck_s, kn_s)
    conv_silu(v_ref, cwv_ref, cv_s, vn_s)
    q = qn_s[...]
    k = kn_s[...]
    qn_s[...] = q * lax.rsqrt(jnp.sum(q * q, -1, keepdims=True) + RMS_EPS) * (hd ** -0.5)
    kn_s[...] = k * lax.rsqrt(jnp.sum(k * k, -1, keepdims=True) + RMS_EPS)

    bd = bd_ref[...]
    lane = lax.broadcasted_iota(jnp.int32, bd.shape, 1)
    beta_in = jnp.sum(jnp.where(lane == h, bd, 0.0), -1, keepdims=True)
    dec_in = jnp.sum(jnp.where(lane == GDN_HEADS + h, bd, 0.0), -1, keepdims=True)
    a_log = alog_ref[...][:, 0:1]
    dt_bias = dtb_ref[...][:, 0:1]
    xs = dec_in + dt_bias
    softplus = jnp.maximum(xs, 0.0) + jnp.log(1.0 + jnp.exp(-jnp.abs(xs)))
    g = -jnp.exp(a_log) * softplus
    gc_s[...] = _chunk_cumsum(jnp.broadcast_to(g, (tt, LANES)), LIN_CHUNK)
    bt_s[...] = jnp.broadcast_to(_sigmoid(beta_in), (tt, LANES))

    ri = lax.broadcasted_iota(jnp.int32, (LIN_CHUNK, LIN_CHUNK), 0)
    ci = lax.broadcasted_iota(jnp.int32, (LIN_CHUNK, LIN_CHUNK), 1)
    causal = ri >= ci
    strict = ri > ci

    for g0 in range(0, nc, GDN_CHUNK_GROUP):
        rows = [slice(c * LIN_CHUNK, (c + 1) * LIN_CHUNK) for c in range(g0, g0 + GDN_CHUNK_GROUP)]
        kc = [kn_s[r, :] for r in rows]
        gc = [gc_s[r, :] for r in rows]
        beta = [bt_s[r, :] for r in rows]
        decay = []
        for a in gc:
            diff = a[:, 0:LIN_CHUNK] - a.T[0:LIN_CHUNK, :]
            decay.append(jnp.where(causal, jnp.exp(jnp.where(causal, diff, 0.0)), 0.0))
        kb = [a * b for a, b in zip(kc, beta)]
        kcb = [a.astype(BF16) for a in kc]
        kk = [_dot_nt(a.astype(BF16), b) for a, b in zip(kb, kcb)]
        qk = [_dot_nt(qn_s[r, :].astype(BF16), b) for r, b in zip(rows, kcb)]
        lows = [jnp.where(strict, a * d, 0.0) for a, d in zip(kk, decay)]
        for r, a, d in zip(rows, qk, decay):
            at_s[r, :] = (a * d).astype(BF16)
        xinv = [a.astype(BF16) for a in _unit_lower_inverse_minus_eye(lows)]
        egc = [jnp.exp(a) for a in gc]
        vb = [vn_s[r, :] * b for r, b in zip(rows, beta)]
        kbe = [a * e for a, e in zip(kb, egc)]
        xu = [_dot(x, a.astype(BF16)) for x, a in zip(xinv, vb)]
        xw = [_dot(x, a.astype(BF16)) for x, a in zip(xinv, kbe)]
        for i, r in enumerate(rows):
            c = g0 + i
            u_s[r, :] = vb[i] + xu[i]
            w_s[r, :] = (kbe[i] + xw[i]).astype(BF16)
            qd_s[r, :] = (qn_s[r, :] * egc[i]).astype(BF16)
            g_last = gc[i][LIN_CHUNK - 1:LIN_CHUNK, :]
            kd_s[r, :] = (kc[i] * jnp.exp(g_last - gc[i])).astype(BF16)
            egl_s[c * SUBLANES:(c + 1) * SUBLANES, :] = jnp.broadcast_to(jnp.exp(g_last), (SUBLANES, LANES))


def _gdn_scan_kernel(u_ref, w_ref, at_ref, qd_ref, kd_ref, egl_ref, z_ref, ng_ref, o_ref, st_ref, *, tt):
    nc = tt // LIN_CHUNK
    hd = GDN_HEAD_DIM

    @pl.when(pl.program_id(0) == 0)
    def _():
        st_ref[...] = jnp.zeros_like(st_ref)

    ng = ng_ref[...]

    def scan_body(c, carry):
        r0 = pl.multiple_of(c * LIN_CHUNK, LIN_CHUNK)
        rows = pl.ds(r0, LIN_CHUNK)
        e0 = pl.multiple_of(c * SUBLANES, SUBLANES)
        heads = range(GDN_HEADS)
        cols = [slice(h * hd, (h + 1) * hd) for h in heads]
        st = [st_ref[h] for h in heads]
        stb = [a.astype(BF16) for a in st]
        ws = [_dot(w_ref[rows, c_], b) for c_, b in zip(cols, stb)]
        qs = [_dot(qd_ref[rows, c_], b) for c_, b in zip(cols, stb)]
        vnb = [(u_ref[rows, c_] - a).astype(BF16) for c_, a in zip(cols, ws)]
        av = [_dot(at_ref[h, rows, :], b) for h, b in zip(heads, vnb)]
        kv = [_dot_tn(kd_ref[rows, c_], b) for c_, b in zip(cols, vnb)]
        for h in heads:
            st_ref[h] = st[h] * egl_ref[h, pl.ds(e0, 1), :] + kv[h]
            o = qs[h] + av[h]
            o = o * lax.rsqrt(jnp.mean(o * o, -1, keepdims=True) + RMS_EPS) * ng
            o_ref[rows, cols[h]] = (o * _silu(z_ref[rows, cols[h]])).astype(o_ref.dtype)
        return carry

    lax.fori_loop(0, nc, scan_body, 0)


def _gdn(proj, bd, conv_w, a_log, dt_bias, norm_g, tt_local, tt_scan):
    s = proj.shape[0]
    nh = GDN_HEADS
    hd = GDN_HEAD_DIM
    tt = tt_local
    per_head = lambda a: jnp.broadcast_to(a.astype(F32)[:, None, None], (nh, 1, LANES))
    blk = lambda off: pl.BlockSpec((tt, hd), lambda h, i: (i, off + h))
    cw = lambda off: pl.BlockSpec((GDN_CONV, hd), lambda h, i: (0, off + h))
    scal = pl.BlockSpec((None, 1, LANES), lambda h, i: (h, 0, 0))
    head_blk = pl.BlockSpec((tt, hd), lambda h, i: (i, h))
    u, w, at, qd, kd, egl = pl.pallas_call(
        functools.partial(_gdn_local_kernel, tt=tt),
        grid=(nh, s // tt),
        in_specs=[blk(0), blk(nh), blk(2 * nh),
                  pl.BlockSpec((tt, LANES), lambda h, i: (i, 0)),
                  cw(0), cw(nh), cw(2 * nh), scal, scal],
        out_specs=[head_blk, head_blk,
                   pl.BlockSpec((None, tt, LIN_CHUNK), lambda h, i: (h, i, 0)),
                   head_blk, head_blk,
                   pl.BlockSpec((None, tt // LIN_CHUNK * SUBLANES, LANES), lambda h, i: (h, i, 0))],
        out_shape=[jax.ShapeDtypeStruct((s, nh * hd), F32), jax.ShapeDtypeStruct((s, nh * hd), BF16),
                   jax.ShapeDtypeStruct((nh, s, LIN_CHUNK), BF16),
                   jax.ShapeDtypeStruct((s, nh * hd), BF16), jax.ShapeDtypeStruct((s, nh * hd), BF16),
                   jax.ShapeDtypeStruct((nh, s // LIN_CHUNK * SUBLANES, LANES), F32)],
        scratch_shapes=[pltpu.VMEM((SUBLANES, hd), F32), pltpu.VMEM((SUBLANES, hd), F32),
                        pltpu.VMEM((SUBLANES, hd), F32),
                        pltpu.VMEM((tt, hd), F32), pltpu.VMEM((tt, hd), F32), pltpu.VMEM((tt, hd), F32),
                        pltpu.VMEM((tt, LANES), F32), pltpu.VMEM((tt, LANES), F32)],
        compiler_params=_params(("parallel", "arbitrary")),
        name="gdn_local",
    )(proj, proj, proj, bd, conv_w, conv_w, conv_w, per_head(a_log), per_head(dt_bias))
    tt = tt_scan
    wide = pl.BlockSpec((tt, nh * hd), lambda i: (i, 0))
    return pl.pallas_call(
        functools.partial(_gdn_scan_kernel, tt=tt),
        grid=(s // tt,),
        in_specs=[wide, wide,
                  pl.BlockSpec((nh, tt, LIN_CHUNK), lambda i: (0, i, 0)),
                  wide, wide,
                  pl.BlockSpec((nh, tt // LIN_CHUNK * SUBLANES, LANES), lambda i: (0, i, 0)),
                  pl.BlockSpec((tt, nh * hd), lambda i: (i, 3)),
                  pl.BlockSpec((1, hd), lambda i: (0, 0))],
        out_specs=wide,
        out_shape=jax.ShapeDtypeStruct((s, nh * hd), BF16),
        scratch_shapes=[pltpu.VMEM((nh, hd, hd), F32)],
        compiler_params=_params(("arbitrary",)),
        name="gdn_scan",
    )(u, w, at, qd, kd, egl, proj, norm_g.reshape(1, hd))


S5_LANE_CHUNK = 1024
S5_GROUPS_PER_TILE = LANES // S5_GROUP_CH


def _s5_kernel(u_ref, bre_ref, bim_ref, cre_ref, cim_ref, are_ref, aim_ref, pre_ref, pim_ref, d_ref,
               o_ref, nat_s, up_s, xr_s, xi_s, cr_s, ci_s, gr_s, gi_s, *, tt):
    nseg = SUBLANES
    seg = tt // nseg
    nstate = S5_GROUPS * S5_STATE
    ntile = S5_CHANNELS // LANES
    spt = S5_GROUPS_PER_TILE * S5_STATE

    @pl.when(pl.program_id(0) == 0)
    def _():
        gr_s[...] = jnp.zeros_like(gr_s)
        gi_s[...] = jnp.zeros_like(gi_s)

    for t in range(ntile):
        nat_s[t] = u_ref[:, t * LANES:(t + 1) * LANES]
    for t in range(ntile):
        for j in range(seg):
            up_s[t, j * nseg:(j + 1) * nseg, :] = nat_s[t, pl.ds(j, nseg, stride=seg), :]
    for t in range(ntile):
        ub = up_s[t].astype(BF16)
        xr_s[:, t * spt:(t + 1) * spt] = _dot(ub, bre_ref[t])
        xi_s[:, t * spt:(t + 1) * spt] = _dot(ub, bim_ref[t])

    for lc in range(nstate // S5_LANE_CHUNK):
        cols = slice(lc * S5_LANE_CHUNK, (lc + 1) * S5_LANE_CHUNK)
        ar = are_ref[:, cols]
        ai = aim_ref[:, cols]

        def local_step(j, carry):
            xr, xi = carry
            r0 = pl.multiple_of(j * nseg, nseg)
            nr = ar * xr - ai * xi + xr_s[pl.ds(r0, nseg), cols]
            ni = ar * xi + ai * xr + xi_s[pl.ds(r0, nseg), cols]
            xr_s[pl.ds(r0, nseg), cols] = nr
            xi_s[pl.ds(r0, nseg), cols] = ni
            return nr, ni

        zero = jnp.zeros((nseg, S5_LANE_CHUNK), F32)
        er, ei = lax.fori_loop(0, seg, local_step, (zero, zero))
        tiles = range(lc * S5_LANE_CHUNK // LANES, (lc + 1) * S5_LANE_CHUNK // LANES)
        pr = jnp.concatenate([pre_ref[t, seg - 1:seg, :] for t in tiles], axis=1)
        pi = jnp.concatenate([pim_ref[t, seg - 1:seg, :] for t in tiles], axis=1)
        cr = gr_s[:, cols]
        ci = gi_s[:, cols]
        for r in range(nseg):
            cr_s[r:r + 1, cols] = cr
            ci_s[r:r + 1, cols] = ci
            cr, ci = (pr * cr - pi * ci + er[r:r + 1], pr * ci + pi * cr + ei[r:r + 1])
        gr_s[:, cols] = cr
        gi_s[:, cols] = ci
        c_re = cr_s[:, cols]
        c_im = ci_s[:, cols]

        def fix_step(j, carry):
            r0 = pl.multiple_of(j * nseg, nseg)
            pjr = jnp.concatenate([pre_ref[t, pl.ds(j, 1), :] for t in tiles], axis=1)
            pji = jnp.concatenate([pim_ref[t, pl.ds(j, 1), :] for t in tiles], axis=1)
            xr_s[pl.ds(r0, nseg), cols] = xr_s[pl.ds(r0, nseg), cols] + pjr * c_re - pji * c_im
            xi_s[pl.ds(r0, nseg), cols] = xi_s[pl.ds(r0, nseg), cols] + pjr * c_im + pji * c_re
            return carry

        lax.fori_loop(0, seg, fix_step, 0)

    for t in range(ntile):
        xr = xr_s[:, t * spt:(t + 1) * spt].astype(BF16)
        xi = xi_s[:, t * spt:(t + 1) * spt].astype(BF16)
        y = _dot(xr, cre_ref[t]) - _dot(xi, cim_ref[t]) + d_ref[:, t * LANES:(t + 1) * LANES] * up_s[t]
        up_s[t] = 0.5 * y * (1.0 + jnp.tanh(math.sqrt(2.0 / math.pi) * (y + 0.044715 * (y * y * y))))
    for t in range(ntile):
        for j in range(seg):
            nat_s[t, pl.ds(j, nseg, stride=seg), :] = up_s[t, j * nseg:(j + 1) * nseg, :]
    for t in range(ntile):
        o_ref[:, t * LANES:(t + 1) * LANES] = nat_s[t]


def _s5(proj, col0, a_re, a_im, b_re, b_im, c_re, c_im, d, log_step, tt):
    s = proj.shape[0]
    seg = tt // SUBLANES
    nstate = S5_GROUPS * S5_STATE
    ntile = S5_CHANNELS // LANES
    gpt = S5_GROUPS_PER_TILE
    a_re, a_im, b_re, b_im, c_re, c_im, d = (t.astype(F32) for t in (a_re, a_im, b_re, b_im, c_re, c_im, d))
    step = jnp.exp(log_step.astype(F32))[:, None]
    mag = jnp.exp(a_re * step)
    ab_re = mag * jnp.cos(a_im * step)
    ab_im = mag * jnp.sin(a_im * step)
    den = jnp.square(a_re) + jnp.square(a_im)
    z_re = ((ab_re - 1.0) * a_re + ab_im * a_im) / den
    z_im = (ab_im * a_re - (ab_re - 1.0) * a_im) / den
    bb_re = z_re[..., None] * b_re - z_im[..., None] * b_im
    bb_im = z_re[..., None] * b_im + z_im[..., None] * b_re

    def block_diag_in(bb):
        bt = bb.reshape(ntile, gpt, S5_STATE, S5_GROUP_CH)
        eye = jnp.eye(gpt, dtype=F32)
        m = jnp.einsum('tgpc,gh->tgchp', bt, eye)
        return m.reshape(ntile, gpt * S5_GROUP_CH, gpt * S5_STATE).astype(BF16)

    def block_diag_out(cc):
        ct = cc.reshape(ntile, gpt, S5_GROUP_CH, S5_STATE)
        eye = jnp.eye(gpt, dtype=F32)
        m = jnp.einsum('tgcp,gh->tgphc', ct, eye)
        return m.reshape(ntile, gpt * S5_STATE, gpt * S5_GROUP_CH).astype(BF16)

    pw_re, pw_im = ab_re[None], ab_im[None]
    sr, si = ab_re, ab_im
    while pw_re.shape[0] < seg:
        pw_re, pw_im = (jnp.concatenate([pw_re, pw_re * sr - pw_im * si], axis=0),
                        jnp.concatenate([pw_im, pw_re * si + pw_im * sr], axis=0))
        sr, si = sr * sr - si * si, 2.0 * sr * si
    pw_re = pw_re.reshape(seg, nstate // LANES, LANES).transpose(1, 0, 2)
    pw_im = pw_im.reshape(seg, nstate // LANES, LANES).transpose(1, 0, 2)
    full = lambda shape: pl.BlockSpec(shape, lambda i: (0,) * len(shape))
    return pl.pallas_call(
        functools.partial(_s5_kernel, tt=tt),
        grid=(s // tt,),
        in_specs=[pl.BlockSpec((tt, S5_CHANNELS), lambda i: (i, col0)),
                  full((ntile, LANES, gpt * S5_STATE)), full((ntile, LANES, gpt * S5_STATE)),
                  full((ntile, gpt * S5_STATE, LANES)), full((ntile, gpt * S5_STATE, LANES)),
                  full((1, nstate)), full((1, nstate)),
                  full((nstate // LANES, seg, LANES)), full((nstate // LANES, seg, LANES)),
                  full((1, S5_CHANNELS))],
        out_specs=pl.BlockSpec((tt, S5_CHANNELS), lambda i: (i, 0)),
        out_shape=jax.ShapeDtypeStruct((s, S5_CHANNELS), F32),
        scratch_shapes=[pltpu.VMEM((ntile, tt, LANES), F32), pltpu.VMEM((ntile, tt, LANES), F32),
                        pltpu.VMEM((tt, nstate), F32), pltpu.VMEM((tt, nstate), F32),
                        pltpu.VMEM((SUBLANES, nstate), F32), pltpu.VMEM((SUBLANES, nstate), F32),
                        pltpu.VMEM((1, nstate), F32), pltpu.VMEM((1, nstate), F32)],
        compiler_params=_params(("arbitrary",)),
        name="s5",
    )(proj, block_diag_in(bb_re), block_diag_in(bb_im), block_diag_out(c_re), block_diag_out(c_im),
      ab_re.reshape(1, nstate), ab_im.reshape(1, nstate), pw_re, pw_im, d.reshape(1, S5_CHANNELS))


def _glu_kernel(y_ref, w_ref, b_ref, o_ref):
    y = y_ref[...]
    t = _dot(y.astype(BF16), w_ref[...]) + b_ref[...]
    o_ref[...] = (y * _sigmoid(t)).astype(o_ref.dtype)


def _glu(y, w, b, tm):
    m, n = y.shape
    return pl.pallas_call(
        _glu_kernel,
        grid=(m // tm,),
        in_specs=[pl.BlockSpec((tm, n), lambda i: (i, 0)),
                  pl.BlockSpec((n, n), lambda i: (0, 0)),
                  pl.BlockSpec((1, n), lambda i: (0, 0))],
        out_specs=pl.BlockSpec((tm, n), lambda i: (i, 0)),
        out_shape=jax.ShapeDtypeStruct((m, n), BF16),
        compiler_params=_params(("parallel",)),
        name="glu",
    )(y, w, b.reshape(1, n))


def _pad_cols(w, n):
    return jnp.concatenate([w, jnp.zeros((w.shape[0], n - w.shape[1]), w.dtype)], axis=1)


def _split_cols(w, sizes):
    offs = [0]
    for sz in sizes:
        offs.append(offs[-1] + sz)
    return [w[:, offs[i]:offs[i + 1]] for i in range(len(sizes))]


def _even_mixer(xb, w_in, w_gate2, b_gate, norm_g):
    hk, hv, hm = GLA_HEADS * GLA_DK, GLA_HEADS * GLA_DV, MOBA_HEADS * MOBA_HEAD_DIM
    gq, gk, gv, g_lr, gr, mq, mk, mv = _split_cols(w_in, (hk, hk, hv, GLA_GATE_RANK, hv, hm, hm, hm))
    w_main = jnp.concatenate([gq, gk, gv, gr, mq, mk, mv], axis=1).astype(BF16)
    proj = _matmul(xb, w_main, F32, 1024, 1024)
    glr = _matmul(xb, _pad_cols(g_lr, LANES).astype(BF16), F32, 1024, LANES)
    gla_out = _gla(proj, glr, w_gate2, b_gate, norm_g, 512)
    moba_out = _moba(proj, (2 * hk + 2 * hv) // LANES)
    return gla_out, moba_out


def _odd_mixer(xb, w_in, conv_w, a_log, dt_bias, norm_g, a_re, a_im, b_re, b_im, c_re, c_im, d, log_step,
               glu_w, glu_b):
    hq = GDN_HEADS * GDN_HEAD_DIM
    qkv, z, beta, decay, u = _split_cols(w_in, (3 * hq, hq, GDN_HEADS, GDN_HEADS, S5_CHANNELS))
    w_main = jnp.concatenate([qkv, z, u], axis=1).astype(BF16)
    proj = _matmul(xb, w_main, F32, 1024, 1024)
    bd = _matmul(xb, _pad_cols(jnp.concatenate([beta, decay], axis=1), LANES).astype(BF16), F32, 1024, LANES)
    gdn_out = _gdn(proj, bd, conv_w, a_log, dt_bias, norm_g, 512, 512)
    y = _s5(proj, 4 * hq // S5_CHANNELS, a_re, a_im, b_re, b_im, c_re, c_im, d, log_step, 512)
    s5_out = _glu(y, glu_w.astype(BF16), glu_b, 512)
    return gdn_out, s5_out


def kernel(x, even_w_in, gla_w_gate2, gla_b_gate, gla_norm_g, even_w_out, odd_w_in, gdn_conv_w, gdn_a_log,
           gdn_dt_bias, gdn_norm_g, s5_a_re, s5_a_im, s5_b_re, s5_b_im, s5_c_re, s5_c_im, s5_d, s5_log_step,
           s5_glu_w, s5_glu_b, odd_w_out, ln_mix_g, ln_mix_b, ffn_w_up, ffn_conv_w, ffn_w_down, ln_ffn_g, ln_ffn_b):
    bsz, seq, dm = x.shape
    outs = []
    for b in range(bsz):
        xf = x[b]
        xb = xf.astype(BF16)
        for i in range(DEPTH):
            j = i // 2
            if i % 2 == 0:
                mix_in = _even_mixer(xb, even_w_in[j], gla_w_gate2[j], gla_b_gate[j], gla_norm_g[j])
                w_out = even_w_out[j]
            else:
                mix_in = _odd_mixer(xb, odd_w_in[j], gdn_conv_w[j], gdn_a_log[j], gdn_dt_bias[j], gdn_norm_g[j],
                                    s5_a_re[j], s5_a_im[j], s5_b_re[j], s5_b_im[j], s5_c_re[j], s5_c_im[j],
                                    s5_d[j], s5_log_step[j], s5_glu_w[j], s5_glu_b[j])
                w_out = odd_w_out[j]
            xf, xb = _matmul2_res_ln(mix_in[0], mix_in[1], w_out.astype(BF16), xf, ln_mix_g[i], ln_mix_b[i], 512)
            mid = _ffn_up(xb, ffn_w_up, ffn_conv_w, i, 1024, 512)
            xf, xb = _matmul_res_ln(mid, ffn_w_down[i].astype(BF16), xf, ln_ffn_g[i], ln_ffn_b[i], 512, D_FF // 4)
        outs.append(xf)
    return jnp.stack(outs, axis=0)
```

```python
import functools
import math

import jax
import jax.numpy as jnp
from jax import lax
from jax.experimental import pallas as pl
from jax.experimental.pallas import tpu as pltpu

F32 = jnp.float32
BF16 = jnp.bfloat16
HIGHEST = lax.Precision.HIGHEST

DEPTH = 4
GLA_HEADS, GLA_DK, GLA_DV, GLA_GATE_RANK, GLA_GATE_NORM = 8, 64, 128, 16, 16.0
MOBA_HEADS, MOBA_HEAD_DIM, MOBA_BLOCK, MOBA_TOPK = 8, 128, 256, 3
GDN_HEADS, GDN_HEAD_DIM, GDN_CONV = 8, 128, 4
S5_CHANNELS, S5_GROUP_CH, S5_STATE = 1024, 16, 64
S5_GROUPS = S5_CHANNELS // S5_GROUP_CH
LIN_CHUNK = 64
D_FF = 5632
FFN_CONV = 3
DEEPNORM_ALPHA = (2 * DEPTH) ** 0.25
LN_EPS = 1e-5
RMS_EPS = 1e-6

LANES = 128
SUBLANES = 8
VMEM_LIMIT_BYTES = 56 * 1024 * 1024
NEG_BIG = -1e30


def _params(semantics):
    return pltpu.CompilerParams(dimension_semantics=semantics, vmem_limit_bytes=VMEM_LIMIT_BYTES)


def _dot(a, b, precision=None):
    return jnp.dot(a, b, preferred_element_type=F32, precision=precision)


def _dot_nt(a, b, precision=None):
    return lax.dot_general(a, b, (((1,), (1,)), ((), ())), preferred_element_type=F32, precision=precision)


def _dot_tn(a, b, precision=None):
    return lax.dot_general(a, b, (((0,), (0,)), ((), ())), preferred_element_type=F32, precision=precision)


def _sigmoid(x):
    return 1.0 / (1.0 + jnp.exp(-x))


def _silu(x):
    return x * _sigmoid(x)


def _chunk_cumsum(x, chunk):
    pos = lax.broadcasted_iota(jnp.int32, x.shape, 0) % chunk
    k = 1
    while k < chunk:
        x = x + jnp.where(pos >= k, pltpu.roll(x, k, 0), 0.0)
        k *= 2
    return x


def _mm_kernel(a_ref, b_ref, o_ref):
    o_ref[...] = _dot(a_ref[...].astype(BF16), b_ref[...]).astype(o_ref.dtype)


def _matmul(a, b, out_dtype, tm, tn):
    m, k = a.shape
    n = b.shape[1]
    return pl.pallas_call(
        _mm_kernel,
        grid=(m // tm, n // tn),
        in_specs=[pl.BlockSpec((tm, k), lambda i, j: (i, 0)),
                  pl.BlockSpec((k, tn), lambda i, j: (0, j))],
        out_specs=pl.BlockSpec((tm, tn), lambda i, j: (i, j)),
        out_shape=jax.ShapeDtypeStruct((m, n), out_dtype),
        compiler_params=_params(("parallel", "parallel")),
        name="matmul",
    )(a, b)


def _mm_ln_kernel(a_ref, b_ref, x_ref, g_ref, bt_ref, of_ref, ob_ref, acc_ref, *, nk):
    k = pl.program_id(1)

    @pl.when(k == 0)
    def _():
        acc_ref[...] = jnp.zeros_like(acc_ref)

    acc_ref[...] += _dot(a_ref[...], b_ref[...])

    @pl.when(k == nk - 1)
    def _():
        y = DEEPNORM_ALPHA * x_ref[...] + acc_ref[...]
        mu = jnp.mean(y, -1, keepdims=True)
        d = y - mu
        var = jnp.mean(d * d, -1, keepdims=True)
        o = d * lax.rsqrt(var + LN_EPS) * g_ref[...] + bt_ref[...]
        of_ref[...] = o
        ob_ref[...] = o.astype(BF16)


def _matmul_res_ln(a, b, x, g, bt, tm, tk):
    m, k = a.shape
    n = b.shape[1]
    nk = k // tk
    return pl.pallas_call(
        functools.partial(_mm_ln_kernel, nk=nk),
        grid=(m // tm, nk),
        in_specs=[pl.BlockSpec((tm, tk), lambda i, kk: (i, kk)),
                  pl.BlockSpec((tk, n), lambda i, kk: (kk, 0)),
                  pl.BlockSpec((tm, n), lambda i, kk: (i, 0)),
                  pl.BlockSpec((1, n), lambda i, kk: (0, 0)),
                  pl.BlockSpec((1, n), lambda i, kk: (0, 0))],
        out_specs=[pl.BlockSpec((tm, n), lambda i, kk: (i, 0)),
                   pl.BlockSpec((tm, n), lambda i, kk: (i, 0))],
        out_shape=[jax.ShapeDtypeStruct((m, n), F32), jax.ShapeDtypeStruct((m, n), BF16)],
        scratch_shapes=[pltpu.VMEM((tm, n), F32)],
        compiler_params=_params(("parallel", "arbitrary")),
        name="matmul_res_ln",
    )(a, b, x, g.reshape(1, n), bt.reshape(1, n))


def _mm2_ln_kernel(a1_ref, a2_ref, b_ref, x_ref, g_ref, bt_ref, of_ref, ob_ref):
    k1 = a1_ref.shape[1]
    acc = _dot(a1_ref[...], b_ref[0:k1, :]) + _dot(a2_ref[...], b_ref[k1:, :])
    y = DEEPNORM_ALPHA * x_ref[...] + acc
    mu = jnp.mean(y, -1, keepdims=True)
    d = y - mu
    var = jnp.mean(d * d, -1, keepdims=True)
    o = d * lax.rsqrt(var + LN_EPS) * g_ref[...] + bt_ref[...]
    of_ref[...] = o
    ob_ref[...] = o.astype(BF16)


def _matmul2_res_ln(a1, a2, b, x, g, bt, tm):
    m, k1 = a1.shape
    k2 = a2.shape[1]
    n = b.shape[1]
    row = lambda w: pl.BlockSpec((tm, w), lambda i: (i, 0))
    return pl.pallas_call(
        _mm2_ln_kernel,
        grid=(m // tm,),
        in_specs=[row(k1), row(k2),
                  pl.BlockSpec((k1 + k2, n), lambda i: (0, 0)),
                  row(n),
                  pl.BlockSpec((1, n), lambda i: (0, 0)),
                  pl.BlockSpec((1, n), lambda i: (0, 0))],
        out_specs=[row(n), row(n)],
        out_shape=[jax.ShapeDtypeStruct((m, n), F32), jax.ShapeDtypeStruct((m, n), BF16)],
        compiler_params=_params(("parallel",)),
        name="matmul2_res_ln",
    )(a1, a2, b, x, g.reshape(1, n), bt.reshape(1, n))


def _causal_conv_rows(h, w, prev):
    width = w.shape[0]
    out = w[width - 1:width] * h
    top = jnp.concatenate([prev, h[0:SUBLANES]], axis=0)
    out_top = w[width - 1:width] * top
    for s in range(1, width):
        wj = w[width - 1 - s:width - s]
        out = out + wj * pltpu.roll(h, s, 0)
        out_top = out_top + wj * pltpu.roll(top, s, 0)
    return out, out_top[SUBLANES:]


def _ffn_up_kernel(x_ref, wg_ref, wv_ref, cg_ref, cv_ref, o_ref, carry_g, carry_v, wgb_s, wvb_s):
    @pl.when(pl.program_id(1) == 0)
    def _():
        carry_g[...] = jnp.zeros_like(carry_g)
        carry_v[...] = jnp.zeros_like(carry_v)
        wgb_s[...] = wg_ref[...].astype(BF16)
        wvb_s[...] = wv_ref[...].astype(BF16)

    x = x_ref[...]
    tm = x.shape[0]
    hg = _dot(x, wgb_s[...])
    hv = _dot(x, wvb_s[...])
    g, g_top = _causal_conv_rows(hg, cg_ref[...], carry_g[...])
    v, v_top = _causal_conv_rows(hv, cv_ref[...], carry_v[...])
    carry_g[...] = hg[tm - SUBLANES:]
    carry_v[...] = hv[tm - SUBLANES:]
    o_ref[...] = (_silu(g) * v).astype(o_ref.dtype)
    o_ref[0:SUBLANES, :] = (_silu(g_top) * v_top).astype(o_ref.dtype)


def _ffn_up(xb, w_up, conv_w, layer, tm, tn):
    m, k = xb.shape
    d_ff = w_up.shape[2] // 2
    nt = d_ff // tn
    return pl.pallas_call(
        _ffn_up_kernel,
        grid=(nt, m // tm),
        in_specs=[pl.BlockSpec((tm, k), lambda j, i: (i, 0)),
                  pl.BlockSpec((None, k, tn), lambda j, i: (layer, 0, j)),
                  pl.BlockSpec((None, k, tn), lambda j, i: (layer, 0, j + nt)),
                  pl.BlockSpec((None, FFN_CONV, tn), lambda j, i: (layer, 0, j)),
                  pl.BlockSpec((None, FFN_CONV, tn), lambda j, i: (layer, 0, j + nt))],
        out_specs=pl.BlockSpec((tm, tn), lambda j, i: (i, j)),
        out_shape=jax.ShapeDtypeStruct((m, d_ff), BF16),
        scratch_shapes=[pltpu.VMEM((SUBLANES, tn), F32), pltpu.VMEM((SUBLANES, tn), F32),
                        pltpu.VMEM((k, tn), BF16), pltpu.VMEM((k, tn), BF16)],
        compiler_params=_params(("parallel", "arbitrary")),
        name="ffn_up",
    )(xb, w_up, w_up, conv_w, conv_w)


def _gla_kernel(q_ref, k_ref, v_ref, r_ref, glr_ref, wg2_ref, bg_ref, ng_ref, o_ref,
                st_ref, qe_s, ke_s, kd_s, vb_s, ecl_s, *, tt):
    nc = tt // LIN_CHUNK
    hk = GLA_HEADS * GLA_DK

    @pl.when(pl.program_id(0) == 0)
    def _():
        st_ref[...] = jnp.zeros_like(st_ref)

    z = _dot(glr_ref[...], wg2_ref[...], HIGHEST) + bg_ref[...]
    log_a = (jnp.minimum(z, 0.0) - jnp.log(1.0 + jnp.exp(-jnp.abs(z)))) / GLA_GATE_NORM
    cum = _chunk_cumsum(log_a, LIN_CHUNK)
    q = q_ref[...] * (GLA_DK ** -0.5)
    k = k_ref[...]
    qe_s[...] = (q * jnp.exp(cum)).astype(BF16)
    ke_s[...] = (k * jnp.exp(-cum)).astype(BF16)
    for c in range(nc):
        rows = slice(c * LIN_CHUNK, (c + 1) * LIN_CHUNK)
        cl = cum[(c + 1) * LIN_CHUNK - 1:(c + 1) * LIN_CHUNK, :]
        kd_s[rows, :] = (k[rows] * jnp.exp(cl - cum[rows])).astype(BF16)
        ecl_s[c * SUBLANES:(c + 1) * SUBLANES, :] = jnp.broadcast_to(jnp.exp(cl), (SUBLANES, hk))
    vb_s[...] = v_ref[...].astype(BF16)

    ri = lax.broadcasted_iota(jnp.int32, (LIN_CHUNK, LIN_CHUNK), 0)
    ci = lax.broadcasted_iota(jnp.int32, (LIN_CHUNK, LIN_CHUNK), 1)
    causal = ri >= ci
    lane = lax.broadcasted_iota(jnp.int32, (LIN_CHUNK, LANES), 1)
    lo_half = lane < GLA_DK
    ng = ng_ref[...]

    def chunk_body(c, carry):
        r0 = pl.multiple_of(c * LIN_CHUNK, LIN_CHUNK)
        e0 = pl.multiple_of(c * SUBLANES, SUBLANES)
        heads = range(GLA_HEADS)
        rows = pl.ds(r0, LIN_CHUNK)
        pcol = [slice((h // 2) * LANES, (h // 2 + 1) * LANES) for h in heads]
        vcol = [slice(h * GLA_DV, (h + 1) * GLA_DV) for h in heads]
        keep = [lo_half if h % 2 == 0 else jnp.logical_not(lo_half) for h in heads]
        qm = [jnp.where(keep[h], qe_s[rows, pcol[h]], 0) for h in heads]
        kdm = [jnp.where(keep[h], kd_s[rows, pcol[h]], 0) for h in heads]
        vh = [vb_s[rows, vcol[h]] for h in heads]
        st = [st_ref[h] for h in heads]
        qk = [_dot_nt(qm[h], ke_s[rows, pcol[h]]) for h in heads]
        qs = [_dot_nt(qm[h], st[h].astype(BF16)) for h in heads]
        kv = [_dot_tn(vh[h], kdm[h]) for h in heads]
        av = [_dot(jnp.where(causal, qk[h], 0.0).astype(BF16), vh[h]) for h in heads]
        for h in heads:
            st_ref[h] = st[h] * ecl_s[pl.ds(e0, 1), pcol[h]] + kv[h]
            o = av[h] + qs[h]
            o = o * lax.rsqrt(jnp.mean(o * o, -1, keepdims=True) + RMS_EPS) * ng
            o_ref[rows, vcol[h]] = (o * _silu(r_ref[rows, vcol[h]])).astype(o_ref.dtype)
        return carry

    lax.fori_loop(0, nc, chunk_body, 0)


def _gla(proj, glr, w_gate2, b_gate, norm_g, tt):
    s = proj.shape[0]
    hk = GLA_HEADS * GLA_DK
    hv = GLA_HEADS * GLA_DV
    wg2 = jnp.zeros((LANES, hk), F32).at[:GLA_GATE_RANK].set(w_gate2)
    return pl.pallas_call(
        functools.partial(_gla_kernel, tt=tt),
        grid=(s // tt,),
        in_specs=[pl.BlockSpec((tt, hk), lambda i: (i, 0)),
                  pl.BlockSpec((tt, hk), lambda i: (i, 1)),
                  pl.BlockSpec((tt, hv), lambda i: (i, 1)),
                  pl.BlockSpec((tt, hv), lambda i: (i, 2)),
                  pl.BlockSpec((tt, LANES), lambda i: (i, 0)),
                  pl.BlockSpec((LANES, hk), lambda i: (0, 0)),
                  pl.BlockSpec((1, hk), lambda i: (0, 0)),
                  pl.BlockSpec((1, GLA_DV), lambda i: (0, 0))],
        out_specs=pl.BlockSpec((tt, hv), lambda i: (i, 0)),
        out_shape=jax.ShapeDtypeStruct((s, hv), BF16),
        scratch_shapes=[pltpu.VMEM((GLA_HEADS, GLA_DV, LANES), F32),
                        pltpu.VMEM((tt, hk), BF16), pltpu.VMEM((tt, hk), BF16), pltpu.VMEM((tt, hk), BF16),
                        pltpu.VMEM((tt, hv), BF16),
                        pltpu.VMEM((tt // LIN_CHUNK * SUBLANES, hk), F32)],
        compiler_params=_params(("arbitrary",)),
        name="gla",
    )(proj, proj, proj, proj, glr, wg2, b_gate.reshape(1, hk), norm_g.reshape(1, GLA_DV))


def _split3(x):
    hi = x.astype(BF16).astype(F32)
    rest = x - hi
    mid = rest.astype(BF16).astype(F32)
    lo = (rest - mid).astype(BF16).astype(F32)
    return hi, mid, lo


MOBA_SEL_LANES = 32
MOBA_ROW_LANE = 96
MOBA_COL_LANE = 99
LOG2E = 1.4426950408889634
MOBA_GROUP = 8


def _moba_kernel(q_ref, k_ref, v_ref, slope_ref, o_ref, kmean_s, kx_s, vt_s, vtb_s, qxt_s, qxo_s, s_s, so_s, acc_s,
                 *, n_blk):
    blk = MOBA_BLOCK
    hd = MOBA_HEAD_DIM
    i = pl.program_id(1)
    sl2 = slope_ref[...][:, 0:1] * LOG2E
    s_hi, s_mid, s_lo = _split3(sl2)

    @pl.when(i == 0)
    def _():
        lane = lax.broadcasted_iota(jnp.int32, (blk, LANES), 1)
        key_f = lax.broadcasted_iota(jnp.int32, (blk, LANES), 0).astype(F32)
        k_extra = jnp.where(lane == MOBA_ROW_LANE, -s_hi, jnp.where(lane == MOBA_ROW_LANE + 1, -s_mid,
                  jnp.where(lane == MOBA_ROW_LANE + 2, -s_lo,
                  jnp.where(jnp.logical_and(lane >= MOBA_COL_LANE, lane < MOBA_COL_LANE + 3), key_f, 0.0))))
        kmean_s[...] = jnp.zeros_like(kmean_s)
        for n in range(n_blk):
            rows = slice(n * blk, (n + 1) * blk)
            kb = k_ref[rows, :]
            kmean_s[n:n + 1, :] = jnp.mean(kb, axis=0, keepdims=True)
            onehot = jnp.logical_or(lane == n, jnp.logical_or(lane == MOBA_SEL_LANES + n,
                                                               lane == 2 * MOBA_SEL_LANES + n))
            kx_s[rows, 0:hd] = kb.astype(BF16)
            kx_s[rows, hd:] = jnp.where(onehot, 1.0, k_extra).astype(BF16)
            gcols = slice((n % MOBA_GROUP) * blk, (n % MOBA_GROUP + 1) * blk)
            vt = v_ref[rows, :].T.astype(BF16)
            vt_s[n // MOBA_GROUP, :, gcols] = vt
            vtb_s[n] = vt

    q = q_ref[...]
    nb = lax.broadcasted_iota(jnp.int32, (MOBA_SEL_LANES, blk), 0)
    nb_f = nb.astype(F32)
    gate = _dot_nt(kmean_s[...], q, HIGHEST)
    gate = jnp.where(nb < i, gate, -jnp.inf)
    chosen = jnp.zeros((MOBA_SEL_LANES, blk), jnp.bool_)
    for _ in range(MOBA_TOPK):
        best = jnp.max(gate, axis=0, keepdims=True)
        idx = jnp.min(jnp.where(gate == best, nb_f, float(MOBA_SEL_LANES)), axis=0, keepdims=True)
        hit = nb_f == idx
        chosen = jnp.logical_or(chosen, hit)
        gate = jnp.where(hit, -jnp.inf, gate)
    sel = jnp.where(jnp.logical_and(chosen, nb < i), -sl2 * ((i - nb) * blk).astype(F32), NEG_BIG)
    b_hi, b_mid, b_lo = _split3(sel)
    qry_f = lax.broadcasted_iota(jnp.int32, (MOBA_SEL_LANES, blk), 1).astype(F32)
    tail = jnp.where(nb < 3, qry_f, jnp.where(nb == 3, s_hi, jnp.where(nb == 4, s_mid,
           jnp.where(nb == 5, s_lo, 0.0))))
    qt = (q * (hd ** -0.5 * LOG2E)).T.astype(BF16)
    qxt_s[0:hd, :] = qt
    qxt_s[hd:, :] = jnp.concatenate([b_hi, b_mid, b_lo, tail], axis=0).astype(BF16)
    qxo_s[0:hd, :] = qt
    qxo_s[hd:, :] = jnp.concatenate([jnp.zeros((3 * MOBA_SEL_LANES, blk), F32), tail], axis=0).astype(BF16)

    def fold(x):
        return x.reshape(x.shape[0] // SUBLANES, SUBLANES, blk)

    key = lax.broadcasted_iota(jnp.int32, (blk, blk), 0)
    qry = lax.broadcasted_iota(jnp.int32, (blk, blk), 1)
    own0 = pl.multiple_of(i * blk, blk)
    s_own = jnp.where(key <= qry, _dot(kx_s[pl.ds(own0, blk), :], qxo_s[...]), NEG_BIG)
    so_s[...] = s_own

    ngrp = (i + MOBA_GROUP - 1) // MOBA_GROUP
    gkeys = MOBA_GROUP * blk

    def score_group(g, mx):
        k0 = pl.multiple_of(g * gkeys, gkeys)
        s = _dot(kx_s[pl.ds(k0, gkeys), :], qxt_s[...])
        s_s[pl.ds(g * MOBA_GROUP, MOBA_GROUP)] = s.reshape(MOBA_GROUP, blk, blk)
        return jnp.maximum(mx, jnp.max(fold(s), axis=0))

    mx = lax.fori_loop(0, ngrp, score_group, jnp.max(fold(s_own), axis=0))
    m_row = jnp.max(mx, axis=0, keepdims=True)

    p_own = jnp.exp2(so_s[...] - m_row)
    acc_s[...] = _dot(vtb_s[i], p_own.astype(BF16))

    def prob_group(g, ls):
        s = s_s[pl.ds(g * MOBA_GROUP, MOBA_GROUP)].reshape(gkeys, blk)
        p = jnp.exp2(s - m_row)
        acc_s[...] += _dot(vt_s[g], p.astype(BF16))
        return ls + jnp.sum(fold(p), axis=0)

    ls = lax.fori_loop(0, ngrp, prob_group, jnp.sum(fold(p_own), axis=0))
    l_row = jnp.sum(ls, axis=0, keepdims=True)
    o_ref[...] = (acc_s[...] / l_row).T.astype(o_ref.dtype)


def _moba(proj, col0):
    s = proj.shape[0]
    n_blk = s // MOBA_BLOCK
    assert n_blk <= MOBA_SEL_LANES
    slopes = jnp.exp2(-8.0 * jnp.arange(1, MOBA_HEADS + 1, dtype=F32) / MOBA_HEADS)
    slopes = jnp.broadcast_to(slopes[:, None, None], (MOBA_HEADS, 1, LANES))
    return pl.pallas_call(
        functools.partial(_moba_kernel, n_blk=n_blk),
        grid=(MOBA_HEADS, n_blk),
        in_specs=[pl.BlockSpec((MOBA_BLOCK, MOBA_HEAD_DIM), lambda h, i: (i, col0 + h)),
                  pl.BlockSpec((s, MOBA_HEAD_DIM), lambda h, i: (0, col0 + MOBA_HEADS + h)),
                  pl.BlockSpec((s, MOBA_HEAD_DIM), lambda h, i: (0, col0 + 2 * MOBA_HEADS + h)),
                  pl.BlockSpec((None, 1, LANES), lambda h, i: (h, 0, 0))],
        out_specs=pl.BlockSpec((MOBA_BLOCK, MOBA_HEAD_DIM), lambda h, i: (i, h)),
        out_shape=jax.ShapeDtypeStruct((s, MOBA_HEADS * MOBA_HEAD_DIM), BF16),
        scratch_shapes=[pltpu.VMEM((MOBA_SEL_LANES, MOBA_HEAD_DIM), F32),
                        pltpu.VMEM((s, MOBA_HEAD_DIM + LANES), BF16),
                        pltpu.VMEM((n_blk // MOBA_GROUP, MOBA_HEAD_DIM, MOBA_GROUP * MOBA_BLOCK), BF16),
                        pltpu.VMEM((n_blk, MOBA_HEAD_DIM, MOBA_BLOCK), BF16),
                        pltpu.VMEM((MOBA_HEAD_DIM + LANES, MOBA_BLOCK), BF16),
                        pltpu.VMEM((MOBA_HEAD_DIM + LANES, MOBA_BLOCK), BF16),
                        pltpu.VMEM((n_blk, MOBA_BLOCK, MOBA_BLOCK), F32),
                        pltpu.VMEM((MOBA_BLOCK, MOBA_BLOCK), F32),
                        pltpu.VMEM((MOBA_HEAD_DIM, MOBA_BLOCK), F32)],
        compiler_params=_params(("parallel", "arbitrary")),
        name="moba",
    )(proj, proj, proj, slopes)


GDN_CHUNK_GROUP = 8


def _unit_lower_inverse_minus_eye(lows):
    n = lows[0].shape[0]
    r = lax.broadcasted_iota(jnp.int32, (n, n), 0)
    c = lax.broadcasted_iota(jnp.int32, (n, n), 1)
    eye = jnp.where(r == c, 1.0, 0.0)
    same16 = (r // 16) == (c // 16)
    same32 = (r // 32) == (c // 32)
    bdot = lambda a, b: [_dot(x.astype(BF16), y.astype(BF16)) for x, y in zip(a, b)]
    ld = [jnp.where(same16, m, 0.0) for m in lows]
    p2 = bdot(ld, ld)
    p4 = bdot(p2, p2)
    p8 = bdot(p4, p4)
    lp = bdot(ld, p2)
    x = [a - b - c_ for a, b, c_ in zip(p2, ld, lp)]
    xp = bdot(x, p4)
    x = [a + b + c_ for a, b, c_ in zip(x, p4, xp)]
    xp = bdot(x, p8)
    t = [eye + a + b + c_ for a, b, c_ in zip(x, p8, xp)]
    c32 = [jnp.where(jnp.logical_and(same32, jnp.logical_not(same16)), m, 0.0) for m in lows]
    t = [a - b for a, b in zip(t, bdot(bdot(t, c32), t))]
    c64 = [jnp.where(same32, 0.0, m) for m in lows]
    t = [a - b for a, b in zip(t, bdot(bdot(t, c64), t))]
    return [a - eye for a in t]


def _gdn_local_kernel(q_ref, k_ref, v_ref, bd_ref, cwq_ref, cwk_ref, cwv_ref, alog_ref, dtb_ref,
                      u_s, w_s, at_s, qd_s, kd_s, egl_s, cq_s, ck_s, cv_s, qn_s, kn_s, vn_s, gc_s, bt_s, *, tt):
    nc = tt // LIN_CHUNK
    hd = GDN_HEAD_DIM
    h = pl.program_id(0)

    @pl.when(pl.program_id(1) == 0)
    def _():
        cq_s[...] = jnp.zeros_like(cq_s)
        ck_s[...] = jnp.zeros_like(ck_s)
        cv_s[...] = jnp.zeros_like(cv_s)

    def conv_silu(x_ref, cw_ref, carry, dst):
        x = x_ref[...]
        y, y_top = _causal_conv_rows(x, cw_ref[...], carry[...])
        carry[...] = x[tt - SUBLANES:]
        dst[...] = _silu(y)
        dst[0:SUBLANES, :] = _silu(y_top)

    conv_silu(q_ref, cwq_ref, cq_s, qn_s)
    conv_silu(k_ref, cwk_ref, ck_s, kn_s)
    conv_silu(v_ref, cwv_ref, cv_s, vn_s)
    q = qn_s[...]
    k = kn_s[...]
    qn_s[...] = q * lax.rsqrt(jnp.sum(q * q, -1, keepdims=True) + RMS_EPS) * (hd ** -0.5)
    kn_s[...] = k * lax.rsqrt(jnp.sum(k * k, -1, keepdims=True) + RMS_EPS)

    bd = bd_ref[...]
    lane = lax.broadcasted_iota(jnp.int32, bd.shape, 1)
    beta_in = jnp.sum(jnp.where(lane == h, bd, 0.0), -1, keepdims=True)
    dec_in = jnp.sum(jnp.where(lane == GDN_HEADS + h, bd, 0.0), -1, keepdims=True)
    a_log = alog_ref[...][:, 0:1]
    dt_bias = dtb_ref[...][:, 0:1]
    xs = dec_in + dt_bias
    softplus = jnp.maximum(xs, 0.0) + jnp.log(1.0 + jnp.exp(-jnp.abs(xs)))
    g = -jnp.exp(a_log) * softplus
    gc_s[...] = _chunk_cumsum(jnp.broadcast_to(g, (tt, LANES)), LIN_CHUNK)
    bt_s[...] = jnp.broadcast_to(_sigmoid(beta_in), (tt, LANES))

    ri = lax.broadcasted_iota(jnp.int32, (LIN_CHUNK, LIN_CHUNK), 0)
    ci = lax.broadcasted_iota(jnp.int32, (LIN_CHUNK, LIN_CHUNK), 1)
    causal = ri >= ci
    strict = ri > ci

    for g0 in range(0, nc, GDN_CHUNK_GROUP):
        rows = [slice(c * LIN_CHUNK, (c + 1) * LIN_CHUNK) for c in range(g0, g0 + GDN_CHUNK_GROUP)]
        kc = [kn_s[r, :] for r in rows]
        gc = [gc_s[r, :] for r in rows]
        beta = [bt_s[r, :] for r in rows]
        decay = []
        for a in gc:
            diff = a[:, 0:LIN_CHUNK] - a.T[0:LIN_CHUNK, :]
            decay.append(jnp.where(causal, jnp.exp(jnp.where(causal, diff, 0.0)), 0.0))
        kb = [a * b for a, b in zip(kc, beta)]
        kcb = [a.astype(BF16) for a in kc]
        kk = [_dot_nt(a.astype(BF16), b) for a, b in zip(kb, kcb)]
        qk = [_dot_nt(qn_s[r, :].astype(BF16), b) for r, b in zip(rows, kcb)]
        lows = [jnp.where(strict, a * d, 0.0) for a, d in zip(kk, decay)]
        for r, a, d in zip(rows, qk, decay):
            at_s[r, :] = (a * d).astype(BF16)
        xinv = [a.astype(BF16) for a in _unit_lower_inverse_minus_eye(lows)]
        egc = [jnp.exp(a) for a in gc]
        vb = [vn_s[r, :] * b for r, b in zip(rows, beta)]
        kbe = [a * e for a, e in zip(kb, egc)]
        xu = [_dot(x, a.astype(BF16)) for x, a in zip(xinv, vb)]
        xw = [_dot(x, a.astype(BF16)) for x, a in zip(xinv, kbe)]
        for i, r in enumerate(rows):
            c = g0 + i
            u_s[r, :] = vb[i] + xu[i]
            w_s[r, :] = (kbe[i] + xw[i]).astype(BF16)
            qd_s[r, :] = (qn_s[r, :] * egc[i]).astype(BF16)
            g_last = gc[i][LIN_CHUNK - 1:LIN_CHUNK, :]
            kd_s[r, :] = (kc[i] * jnp.exp(g_last - gc[i])).astype(BF16)
            egl_s[c * SUBLANES:(c + 1) * SUBLANES, :] = jnp.broadcast_to(jnp.exp(g_last), (SUBLANES, LANES))


def _gdn_scan_kernel(u_ref, w_ref, at_ref, qd_ref, kd_ref, egl_ref, z_ref, ng_ref, o_ref, st_ref, *, tt):
    nc = tt // LIN_CHUNK
    hd = GDN_HEAD_DIM

    @pl.when(pl.program_id(0) == 0)
    def _():
        st_ref[...] = jnp.zeros_like(st_ref)

    ng = ng_ref[...]

    def scan_body(c, carry):
        r0 = pl.multiple_of(c * LIN_CHUNK, LIN_CHUNK)
        rows = pl.ds(r0, LIN_CHUNK)
        e0 = pl.multiple_of(c * SUBLANES, SUBLANES)
        heads = range(GDN_HEADS)
        cols = [slice(h * hd, (h + 1) * hd) for h in heads]
        st = [st_ref[h] for h in heads]
        stb = [a.astype(BF16) for a in st]
        ws = [_dot(w_ref[rows, c_], b) for c_, b in zip(cols, stb)]
        qs = [_dot(qd_ref[rows, c_], b) for c_, b in zip(cols, stb)]
        vnb = [(u_ref[rows, c_] - a).astype(BF16) for c_, a in zip(cols, ws)]
        av = [_dot(at_ref[h, rows, :], b) for h, b in zip(heads, vnb)]
        kv = [_dot_tn(kd_ref[rows, c_], b) for c_, b in zip(cols, vnb)]
        for h in heads:
            st_ref[h] = st[h] * egl_ref[h, pl.ds(e0, 1), :] + kv[h]
            o = qs[h] + av[h]
            o = o * lax.rsqrt(jnp.mean(o * o, -1, keepdims=True) + RMS_EPS) * ng
            o_ref[rows, cols[h]] = (o * _silu(z_ref[rows, cols[h]])).astype(o_ref.dtype)
        return carry

    lax.fori_loop(0, nc, scan_body, 0)


def _gdn(proj, bd, conv_w, a_log, dt_bias, norm_g, tt_local, tt_scan):
    s = proj.shape[0]
    nh = GDN_HEADS
    hd = GDN_HEAD_DIM
    tt = tt_local
    per_head = lambda a: jnp.broadcast_to(a.astype(F32)[:, None, None], (nh, 1, LANES))
    blk = lambda off: pl.BlockSpec((tt, hd), lambda h, i: (i, off + h))
    cw = lambda off: pl.BlockSpec((GDN_CONV, hd), lambda h, i: (0, off + h))
    scal = pl.BlockSpec((None, 1, LANES), lambda h, i: (h, 0, 0))
    head_blk = pl.BlockSpec((tt, hd), lambda h, i: (i, h))
    u, w, at, qd, kd, egl = pl.pallas_call(
        functools.partial(_gdn_local_kernel, tt=tt),
        grid=(nh, s // tt),
        in_specs=[blk(0), blk(nh), blk(2 * nh),
                  pl.BlockSpec((tt, LANES), lambda h, i: (i, 0)),
                  cw(0), cw(nh), cw(2 * nh), scal, scal],
        out_specs=[head_blk, head_blk,
                   pl.BlockSpec((None, tt, LIN_CHUNK), lambda h, i: (h, i, 0)),
                   head_blk, head_blk,
                   pl.BlockSpec((None, tt // LIN_CHUNK * SUBLANES, LANES), lambda h, i: (h, i, 0))],
        out_shape=[jax.ShapeDtypeStruct((s, nh * hd), F32), jax.ShapeDtypeStruct((s, nh * hd), BF16),
                   jax.ShapeDtypeStruct((nh, s, LIN_CHUNK), BF16),
                   jax.ShapeDtypeStruct((s, nh * hd), BF16), jax.ShapeDtypeStruct((s, nh * hd), BF16),
                   jax.ShapeDtypeStruct((nh, s // LIN_CHUNK * SUBLANES, LANES), F32)],
        scratch_shapes=[pltpu.VMEM((SUBLANES, hd), F32), pltpu.VMEM((SUBLANES, hd), F32),
                        pltpu.VMEM((SUBLANES, hd), F32),
                        pltpu.VMEM((tt, hd), F32), pltpu.VMEM((tt, hd), F32), pltpu.VMEM((tt, hd), F32),
                        pltpu.VMEM((tt, LANES), F32), pltpu.VMEM((tt, LANES), F32)],
        compiler_params=_params(("parallel", "arbitrary")),
        name="gdn_local",
    )(proj, proj, proj, bd, conv_w, conv_w, conv_w, per_head(a_log), per_head(dt_bias))
    tt = tt_scan
    wide = pl.BlockSpec((tt, nh * hd), lambda i: (i, 0))
    return pl.pallas_call(
        functools.partial(_gdn_scan_kernel, tt=tt),
        grid=(s // tt,),
        in_specs=[wide, wide,
                  pl.BlockSpec((nh, tt, LIN_CHUNK), lambda i: (0, i, 0)),
                  wide, wide,
                  pl.BlockSpec((nh, tt // LIN_CHUNK * SUBLANES, LANES), lambda i: (0, i, 0)),
                  pl.BlockSpec((tt, nh * hd), lambda i: (i, 3)),
                  pl.BlockSpec((1, hd), lambda i: (0, 0))],
        out_specs=wide,
        out_shape=jax.ShapeDtypeStruct((s, nh * hd), BF16),
        scratch_shapes=[pltpu.VMEM((nh, hd, hd), F32)],
        compiler_params=_params(("arbitrary",)),
        name="gdn_scan",
    )(u, w, at, qd, kd, egl, proj, norm_g.reshape(1, hd))


S5_LANE_CHUNK = 1024
S5_GROUPS_PER_TILE = LANES // S5_GROUP_CH


def _s5_kernel(u_ref, bre_ref, bim_ref, cre_ref, cim_ref, are_ref, aim_ref, pre_ref, pim_ref, d_ref,
               o_ref, nat_s, up_s, xr_s, xi_s, cr_s, ci_s, gr_s, gi_s, *, tt):
    nseg = SUBLANES
    seg = tt // nseg
    nstate = S5_GROUPS * S5_STATE
    ntile = S5_CHANNELS // LANES
    spt = S5_GROUPS_PER_TILE * S5_STATE

    @pl.when(pl.program_id(0) == 0)
    def _():
        gr_s[...] = jnp.zeros_like(gr_s)
        gi_s[...] = jnp.zeros_like(gi_s)

    for t in range(ntile):
        nat_s[t] = u_ref[:, t * LANES:(t + 1) * LANES]
    for t in range(ntile):
        for j in range(seg):
            up_s[t, j * nseg:(j + 1) * nseg, :] = nat_s[t, pl.ds(j, nseg, stride=seg), :]
    for t in range(ntile):
        ub = up_s[t].astype(BF16)
        xr_s[:, t * spt:(t + 1) * spt] = _dot(ub, bre_ref[t])
        xi_s[:, t * spt:(t + 1) * spt] = _dot(ub, bim_ref[t])

    for lc in range(nstate // S5_LANE_CHUNK):
        cols = slice(lc * S5_LANE_CHUNK, (lc + 1) * S5_LANE_CHUNK)
        ar = are_ref[:, cols]
        ai = aim_ref[:, cols]

        def local_step(j, carry):
            xr, xi = carry
            r0 = pl.multiple_of(j * nseg, nseg)
            nr = ar * xr - ai * xi + xr_s[pl.ds(r0, nseg), cols]
            ni = ar * xi + ai * xr + xi_s[pl.ds(r0, nseg), cols]
            xr_s[pl.ds(r0, nseg), cols] = nr
            xi_s[pl.ds(r0, nseg), cols] = ni
            return nr, ni

        zero = jnp.zeros((nseg, S5_LANE_CHUNK), F32)
        er, ei = lax.fori_loop(0, seg, local_step, (zero, zero))
        tiles = range(lc * S5_LANE_CHUNK // LANES, (lc + 1) * S5_LANE_CHUNK // LANES)
        pr = jnp.concatenate([pre_ref[t, seg - 1:seg, :] for t in tiles], axis=1)
        pi = jnp.concatenate([pim_ref[t, seg - 1:seg, :] for t in tiles], axis=1)
        cr = gr_s[:, cols]
        ci = gi_s[:, cols]
        for r in range(nseg):
            cr_s[r:r + 1, cols] = cr
            ci_s[r:r + 1, cols] = ci
            cr, ci = (pr * cr - pi * ci + er[r:r + 1], pr * ci + pi * cr + ei[r:r + 1])
        gr_s[:, cols] = cr
        gi_s[:, cols] = ci
        c_re = cr_s[:, cols]
        c_im = ci_s[:, cols]

        def fix_step(j, carry):
            r0 = pl.multiple_of(j * nseg, nseg)
            pjr = jnp.concatenate([pre_ref[t, pl.ds(j, 1), :] for t in tiles], axis=1)
            pji = jnp.concatenate([pim_ref[t, pl.ds(j, 1), :] for t in tiles], axis=1)
            xr_s[pl.ds(r0, nseg), cols] = xr_s[pl.ds(r0, nseg), cols] + pjr * c_re - pji * c_im
            xi_s[pl.ds(r0, nseg), cols] = xi_s[pl.ds(r0, nseg), cols] + pjr * c_im + pji * c_re
            return carry

        lax.fori_loop(0, seg, fix_step, 0)

    for t in range(ntile):
        xr = xr_s[:, t * spt:(t + 1) * spt].astype(BF16)
        xi = xi_s[:, t * spt:(t + 1) * spt].astype(BF16)
        y = _dot(xr, cre_ref[t]) - _dot(xi, cim_ref[t]) + d_ref[:, t * LANES:(t + 1) * LANES] * up_s[t]
        up_s[t] = 0.5 * y * (1.0 + jnp.tanh(math.sqrt(2.0 / math.pi) * (y + 0.044715 * (y * y * y))))
    for t in range(ntile):
        for j in range(seg):
            nat_s[t, pl.ds(j, nseg, stride=seg), :] = up_s[t, j * nseg:(j + 1) * nseg, :]
    for t in range(ntile):
        o_ref[:, t * LANES:(t + 1) * LANES] = nat_s[t]


def _s5(proj, col0, a_re, a_im, b_re, b_im, c_re, c_im, d, log_step, tt):
    s = proj.shape[0]
    seg = tt // SUBLANES
    nstate = S5_GROUPS * S5_STATE
    ntile = S5_CHANNELS // LANES
    gpt = S5_GROUPS_PER_TILE
    a_re, a_im, b_re, b_im, c_re, c_im, d = (t.astype(F32) for t in (a_re, a_im, b_re, b_im, c_re, c_im, d))
    step = jnp.exp(log_step.astype(F32))[:, None]
    mag = jnp.exp(a_re * step)
    ab_re = mag * jnp.cos(a_im * step)
    ab_im = mag * jnp.sin(a_im * step)
    den = jnp.square(a_re) + jnp.square(a_im)
    z_re = ((ab_re - 1.0) * a_re + ab_im * a_im) / den
    z_im = (ab_im * a_re - (ab_re - 1.0) * a_im) / den
    bb_re = z_re[..., None] * b_re - z_im[..., None] * b_im
    bb_im = z_re[..., None] * b_im + z_im[..., None] * b_re

    def block_diag_in(bb):
        bt = bb.reshape(ntile, gpt, S5_STATE, S5_GROUP_CH)
        eye = jnp.eye(gpt, dtype=F32)
        m = jnp.einsum('tgpc,gh->tgchp', bt, eye)
        return m.reshape(ntile, gpt * S5_GROUP_CH, gpt * S5_STATE).astype(BF16)

    def block_diag_out(cc):
        ct = cc.reshape(ntile, gpt, S5_GROUP_CH, S5_STATE)
        eye = jnp.eye(gpt, dtype=F32)
        m = jnp.einsum('tgcp,gh->tgphc', ct, eye)
        return m.reshape(ntile, gpt * S5_STATE, gpt * S5_GROUP_CH).astype(BF16)

    pw_re, pw_im = ab_re[None], ab_im[None]
    sr, si = ab_re, ab_im
    while pw_re.shape[0] < seg:
        pw_re, pw_im = (jnp.concatenate([pw_re, pw_re * sr - pw_im * si], axis=0),
                        jnp.concatenate([pw_im, pw_re * si + pw_im * sr], axis=0))
        sr, si = sr * sr - si * si, 2.0 * sr * si
    pw_re = pw_re.reshape(seg, nstate // LANES, LANES).transpose(1, 0, 2)
    pw_im = pw_im.reshape(seg, nstate // LANES, LANES).transpose(1, 0, 2)
    full = lambda shape: pl.BlockSpec(shape, lambda i: (0,) * len(shape))
    return pl.pallas_call(
        functools.partial(_s5_kernel, tt=tt),
        grid=(s // tt,),
        in_specs=[pl.BlockSpec((tt, S5_CHANNELS), lambda i: (i, col0)),
                  full((ntile, LANES, gpt * S5_STATE)), full((ntile, LANES, gpt * S5_STATE)),
                  full((ntile, gpt * S5_STATE, LANES)), full((ntile, gpt * S5_STATE, LANES)),
                  full((1, nstate)), full((1, nstate)),
                  full((nstate // LANES, seg, LANES)), full((nstate // LANES, seg, LANES)),
                  full((1, S5_CHANNELS))],
        out_specs=pl.BlockSpec((tt, S5_CHANNELS), lambda i: (i, 0)),
        out_shape=jax.ShapeDtypeStruct((s, S5_CHANNELS), F32),
        scratch_shapes=[pltpu.VMEM((ntile, tt, LANES), F32), pltpu.VMEM((ntile, tt, LANES), F32),
                        pltpu.VMEM((tt, nstate), F32), pltpu.VMEM((tt, nstate), F32),
                        pltpu.VMEM((SUBLANES, nstate), F32), pltpu.VMEM((SUBLANES, nstate), F32),
                        pltpu.VMEM((1, nstate), F32), pltpu.VMEM((1, nstate), F32)],
        compiler_params=_params(("arbitrary",)),
        name="s5",
    )(proj, block_diag_in(bb_re), block_diag_in(bb_im), block_diag_out(c_re), block_diag_out(c_im),
      ab_re.reshape(1, nstate), ab_im.reshape(1, nstate), pw_re, pw_im, d.reshape(1, S5_CHANNELS))


def _glu_kernel(y_ref, w_ref, b_ref, o_ref):
    y = y_ref[...]
    t = _dot(y.astype(BF16), w_ref[...]) + b_ref[...]
    o_ref[...] = (y * _sigmoid(t)).astype(o_ref.dtype)


def _glu(y, w, b, tm):
    m, n = y.shape
    return pl.pallas_call(
        _glu_kernel,
        grid=(m // tm,),
        in_specs=[pl.BlockSpec((tm, n), lambda i: (i, 0)),
                  pl.BlockSpec((n, n), lambda i: (0, 0)),
                  pl.BlockSpec((1, n), lambda i: (0, 0))],
        out_specs=pl.BlockSpec((tm, n), lambda i: (i, 0)),
        out_shape=jax.ShapeDtypeStruct((m, n), BF16),
        compiler_params=_params(("parallel",)),
        name="glu",
    )(y, w, b.reshape(1, n))


def _pad_cols(w, n):
    return jnp.concatenate([w, jnp.zeros((w.shape[0], n - w.shape[1]), w.dtype)], axis=1)


def _split_cols(w, sizes):
    offs = [0]
    for sz in sizes:
        offs.append(offs[-1] + sz)
    return [w[:, offs[i]:offs[i + 1]] for i in range(len(sizes))]


def _even_mixer(xb, w_in, w_gate2, b_gate, norm_g):
    hk, hv, hm = GLA_HEADS * GLA_DK, GLA_HEADS * GLA_DV, MOBA_HEADS * MOBA_HEAD_DIM
    gq, gk, gv, g_lr, gr, mq, mk, mv = _split_cols(w_in, (hk, hk, hv, GLA_GATE_RANK, hv, hm, hm, hm))
    w_main = jnp.concatenate([gq, gk, gv, gr, mq, mk, mv], axis=1).astype(BF16)
    proj = _matmul(xb, w_main, F32, 1024, 1024)
    glr = _matmul(xb, _pad_cols(g_lr, LANES).astype(BF16), F32, 1024, LANES)
    gla_out = _gla(proj, glr, w_gate2, b_gate, norm_g, 512)
    moba_out = _moba(proj, (2 * hk + 2 * hv) // LANES)
    return gla_out, moba_out


def _odd_mixer(xb, w_in, conv_w, a_log, dt_bias, norm_g, a_re, a_im, b_re, b_im, c_re, c_im, d, log_step,
               glu_w, glu_b):
    hq = GDN_HEADS * GDN_HEAD_DIM
    qkv, z, beta, decay, u = _split_cols(w_in, (3 * hq, hq, GDN_HEADS, GDN_HEADS, S5_CHANNELS))
    w_main = jnp.concatenate([qkv, z, u], axis=1).astype(BF16)
    proj = _matmul(xb, w_main, F32, 1024, 1024)
    bd = _matmul(xb, _pad_cols(jnp.concatenate([beta, decay], axis=1), LANES).astype(BF16), F32, 1024, LANES)
    gdn_out = _gdn(proj, bd, conv_w, a_log, dt_bias, norm_g, 512, 512)
    y = _s5(proj, 4 * hq // S5_CHANNELS, a_re, a_im, b_re, b_im, c_re, c_im, d, log_step, 512)
    s5_out = _glu(y, glu_w.astype(BF16), glu_b, 512)
    return gdn_out, s5_out


def kernel(x, even_w_in, gla_w_gate2, gla_b_gate, gla_norm_g, even_w_out, odd_w_in, gdn_conv_w, gdn_a_log,
           gdn_dt_bias, gdn_norm_g, s5_a_re, s5_a_im, s5_b_re, s5_b_im, s5_c_re, s5_c_im, s5_d, s5_log_step,
           s5_glu_w, s5_glu_b, odd_w_out, ln_mix_g, ln_mix_b, ffn_w_up, ffn_conv_w, ffn_w_down, ln_ffn_g, ln_ffn_b):
    bsz, seq, dm = x.shape
    outs = []
    for b in range(bsz):
        xf = x[b]
        xb = xf.astype(BF16)
        for i in range(DEPTH):
            j = i // 2
            if i % 2 == 0:
                mix_in = _even_mixer(xb, even_w_in[j], gla_w_gate2[j], gla_b_gate[j], gla_norm_g[j])
                w_out = even_w_out[j]
            else:
                mix_in = _odd_mixer(xb, odd_w_in[j], gdn_conv_w[j], gdn_a_log[j], gdn_dt_bias[j], gdn_norm_g[j],
                                    s5_a_re[j], s5_a_im[j], s5_b_re[j], s5_b_im[j], s5_c_re[j], s5_c_im[j],
                                    s5_d[j], s5_log_step[j], s5_glu_w[j], s5_glu_b[j])
                w_out = odd_w_out[j]
            xf, xb = _matmul2_res_ln(mix_in[0], mix_in[1], w_out.astype(BF16), xf, ln_mix_g[i], ln_mix_b[i], 512)
            mid = _ffn_up(xb, ffn_w_up, ffn_conv_w, i, 1024, 512)
            xf, xb = _matmul_res_ln(mid, ffn_w_down[i].astype(BF16), xf, ln_ffn_g[i], ln_ffn_b[i], 512, D_FF // 4)
        outs.append(xf)
    return jnp.stack(outs, axis=0)
```
